```python
import math, functools
import jax, jax.numpy as jnp
from jax import lax
import numpy as np

D_MODEL = 2048
BATCH = 1
SEQ = 16384
DEPTH = 1
DEC_BATCH = 32
DEC_SEQ = 1
PAST_LEN = 16384
PAGE_SIZE = 128

MIX_W = D_MODEL
RWKV_W = MIX_W // 2
RWKV_HD = 64
RWKV_H = RWKV_W // RWKV_HD
DECAY_LORA = 64
AAA_LORA = 64
GATE_LORA = 160
MOBA_W = MIX_W - RWKV_W
MOBA_HD = 128
MOBA_H = MOBA_W // MOBA_HD
MOBA_BLOCK = 256
MOBA_TOPK = 3
Q_BLOCK = 128
D_FF = 4 * D_MODEL
RMS_EPS = 1e-6
GN_EPS = 64e-5
SHIFT_W = 3 * RWKV_W + DECAY_LORA + AAA_LORA + GATE_LORA
PROJ_W = SHIFT_W + 3 * MOBA_W
RWKV_SPLITS = (RWKV_W, 2 * RWKV_W, 3 * RWKV_W, 3 * RWKV_W + DECAY_LORA,
               3 * RWKV_W + DECAY_LORA + AAA_LORA)

kernel_name = "hymba_rwkv7_moba_decode_step"


def rmsnorm(x, g):
    xf = x.astype(jnp.float32)
    y = xf * lax.rsqrt(jnp.mean(xf * xf, axis=-1, keepdims=True) + RMS_EPS)
    return y.astype(x.dtype) * g


def rwkv7_mix(z, prev_row, s0, mu, w0, w_lora2, a0, a_lora2, g_lora2, k_k, k_a, r_k, lnx_w, lnx_b):
    n, t, _ = z.shape
    zf = z.astype(jnp.float32)
    z_prev = jnp.concatenate([prev_row.astype(jnp.float32)[:, None], zf[:, :-1]], axis=1)
    zs = zf + mu * (z_prev - zf)
    r, k, v, wd, ad, gd = jnp.split(zs, RWKV_SPLITS, axis=-1)
    w_log = -jax.nn.softplus(-(w0 + jnp.tanh(wd) @ w_lora2)) - 0.5
    decay = jnp.exp(-jnp.exp(w_log))
    a = jax.nn.sigmoid(a0 + ad @ a_lora2)
    g = jax.nn.sigmoid(gd) @ g_lora2
    hs = lambda u: u.reshape(n, t, RWKV_H, RWKV_HD)
    kk = hs(k * k_k)
    kk = kk * lax.rsqrt(jnp.maximum(jnp.sum(kk * kk, axis=-1, keepdims=True), 1e-24))
    k = k * (1.0 + (a - 1.0) * k_a)
    r_h, k_h, v_h, w_h, a_h = hs(r), hs(k), hs(v), hs(decay), hs(a)
    b_h = kk * a_h

    def step(s, inp):
        r_t, w_t, k_t, v_t, kk_t, b_t = inp
        sa = jnp.einsum('nhvk,nhk->nhv', s, -kk_t)
        s = (s * w_t[:, :, None, :] + sa[..., None] * b_t[:, :, None, :]
             + v_t[..., None] * k_t[:, :, None, :])
        return s, jnp.einsum('nhvk,nhk->nhv', s, r_t)

    tm = lambda u: jnp.moveaxis(u, 1, 0)
    s_fin, o = lax.scan(step, s0.astype(jnp.float32),
                        (tm(r_h), tm(w_h), tm(k_h), tm(v_h), tm(kk), tm(b_h)))
    o = jnp.moveaxis(o, 0, 1)
    mean = jnp.mean(o, axis=-1, keepdims=True)
    var = jnp.mean(jnp.square(o - mean), axis=-1, keepdims=True)
    o = ((o - mean) * lax.rsqrt(var + GN_EPS) * lnx_w.reshape(RWKV_H, RWKV_HD)
         + lnx_b.reshape(RWKV_H, RWKV_HD))
    o = o + jnp.sum(r_h * k_h * r_k, axis=-1, keepdims=True) * v_h
    out = (o.reshape(n, t, RWKV_W) * g).astype(z.dtype)
    return out, s_fin, z[:, -1]


def block_means(page_sums):
    n, h, p, d = page_sums.shape
    ppb = MOBA_BLOCK // PAGE_SIZE
    ps = jnp.pad(page_sums, ((0, 0), (0, 0), (0, (-p) % ppb), (0, 0)))
    return ps.reshape(n, h, -1, ppb, d).sum(axis=3) / MOBA_BLOCK


def moba_attend(q, t, km, fetch):
    n, h, nq, d = q.shape
    nb = km.shape[2]
    ksel = min(MOBA_TOPK, nb)
    cb = t // MOBA_BLOCK
    gate = jnp.einsum('nhqd,nhjd->nhqj', q.astype(jnp.float32), km)
    gate = jnp.where(jnp.arange(nb) < cb[:, None], gate, -jnp.inf)
    _, sel = lax.top_k(gate, ksel)
    own = jnp.broadcast_to(cb[:, None], (n, h, nq, 1)).astype(sel.dtype)
    idx = jnp.concatenate([sel, own], axis=-1)
    slot_ok = jnp.concatenate([jnp.arange(ksel) < cb[:, None],
                               jnp.ones((nq, 1), dtype=bool)], axis=-1)
    kg, vg = fetch(idx)
    pos = idx[..., None] * MOBA_BLOCK + jnp.arange(MOBA_BLOCK)
    mask = slot_ok[:, :, None] & (pos <= t[:, None, None])
    s = jnp.einsum('nhqd,nhqsrd->nhqsr', q, kg).astype(jnp.float32) * (d ** -0.5)
    s = jnp.where(mask, s, -jnp.inf)
    p = jax.nn.softmax(s.reshape(n, h, nq, -1), axis=-1).reshape(s.shape)
    return jnp.einsum('nhqsr,nhqsrd->nhqd', p.astype(vg.dtype), vg)


def moba_prompt(q, k, v):
    b, h, s, d = q.shape
    km = block_means(jnp.sum(k.reshape(b, h, s // PAGE_SIZE, PAGE_SIZE, d), axis=3, dtype=jnp.float32))
    nb = km.shape[2]
    pad = ((0, 0), (0, 0), (0, nb * MOBA_BLOCK - s), (0, 0))
    kp = jnp.pad(k, pad).reshape(b, h, nb, MOBA_BLOCK, d)
    vp = jnp.pad(v, pad).reshape(b, h, nb, MOBA_BLOCK, d)
    bi = jnp.arange(b)[:, None, None, None]
    hi = jnp.arange(h)[None, :, None, None]
    fetch = lambda idx: (kp[bi, hi, idx], vp[bi, hi, idx])
    nq = s // Q_BLOCK
    qb = jnp.moveaxis(q.reshape(b, h, nq, Q_BLOCK, d), 2, 0)

    def one(args):
        q_blk, i = args
        return moba_attend(q_blk, i * Q_BLOCK + jnp.arange(Q_BLOCK), km, fetch)

    o = lax.map(one, (qb, jnp.arange(nq)))
    return jnp.moveaxis(o, 0, 2).reshape(b, h, s, d)


def moba_sample(q, k_new, v_new, cache_k, cache_v, pool_sums, page_table, layer):
    db, h, ds, d = q.shape
    n_pages = page_table.shape[1]
    npn = -(-ds // PAGE_SIZE)
    pad = ((0, 0), (0, 0), (0, npn * PAGE_SIZE - ds), (0, 0))
    kn = jnp.pad(k_new, pad).reshape(db, h, npn, PAGE_SIZE, d)
    vn = jnp.pad(v_new, pad).reshape(db, h, npn, PAGE_SIZE, d)
    past_sums = jnp.moveaxis(pool_sums[page_table], 2, 1)
    new_sums = jnp.sum(kn, axis=3, dtype=jnp.float32)
    km = block_means(jnp.concatenate([past_sums, new_sums], axis=2))
    ppb = MOBA_BLOCK // PAGE_SIZE
    bi = jnp.arange(db)[:, None, None, None, None]
    hi = jnp.arange(h)[None, :, None, None, None]

    def fetch(idx):
        lp = idx[..., None] * ppb + jnp.arange(ppb)
        phys = page_table[bi, jnp.clip(lp, 0, n_pages - 1)]
        jn = jnp.clip(lp - n_pages, 0, npn - 1)
        in_past = (lp < n_pages)[..., None, None]
        rows = lambda pool, new: jnp.where(in_past, pool[phys, layer, hi], new[bi, hi, jn]).reshape(
            idx.shape + (MOBA_BLOCK, d))
        return rows(cache_k, kn), rows(cache_v, vn)

    return moba_attend(q, PAST_LEN + jnp.arange(ds), km, fetch)


def head_groups(z, rwkv_prev, rwkv_s0, rwkv_params, moba_fn):
    n, t, _ = z.shape
    o_r, s_fin, last = rwkv7_mix(z[..., :SHIFT_W], rwkv_prev, rwkv_s0, *rwkv_params)
    q, k, v = (jnp.moveaxis(u.reshape(n, t, MOBA_H, MOBA_HD), 2, 1)
               for u in jnp.split(z[..., SHIFT_W:], 3, axis=-1))
    o_m = jnp.moveaxis(moba_fn(q, k, v), 1, 2).reshape(n, t, MOBA_W)
    return jnp.concatenate([o_r, o_m], axis=-1), (k, v, s_fin, last)


def trunk_layer(x, mixer, g1, g2, g3, g4, w_in, w_out, w_up, w_down):
    mixed, new_state = mixer(rmsnorm(x, g1) @ w_in)
    x = x + rmsnorm(mixed @ w_out, g2)
    f = jnp.square(jax.nn.relu(rmsnorm(x, g3) @ w_up)) @ w_down
    return x + rmsnorm(f, g4), new_state


def setup_inputs(seed: int = 0) -> dict:
    key = jax.random.key(seed)
    ks = jax.random.split(key, 32)
    n_pages = PAST_LEN // PAGE_SIZE
    used = DEC_BATCH * n_pages
    n_pool = used + max(1, used // 4)
    nrm = lambda k, shape, s=1.0: jax.random.normal(k, shape, jnp.float32) * s
    gain = lambda k: 1.0 + nrm(k, (DEPTH, D_MODEL), 0.05)
    page_table = jax.random.permutation(ks[0], n_pool)[:used].reshape(DEC_BATCH, n_pages).astype(jnp.int32)
    return {
        "x_prompt": nrm(ks[1], (BATCH, SEQ, D_MODEL)),
        "x_sample": nrm(ks[2], (DEC_BATCH, DEC_SEQ, D_MODEL)),
        "cache_k": nrm(ks[3], (n_pool, DEPTH, MOBA_H, PAGE_SIZE, MOBA_HD)),
        "cache_v": nrm(ks[4], (n_pool, DEPTH, MOBA_H, PAGE_SIZE, MOBA_HD)),
        "page_table": page_table,
        "state_wkv": nrm(ks[5], (DEPTH, DEC_BATCH, RWKV_H, RWKV_HD, RWKV_HD), 0.3),
        "state_shift": nrm(ks[6], (DEPTH, DEC_BATCH, SHIFT_W)),
        "g_mix_pre": gain(ks[7]),
        "g_mix_post": gain(ks[8]),
        "g_ffn_pre": gain(ks[9]),
        "g_ffn_post": gain(ks[10]),
        "w_in": nrm(ks[11], (DEPTH, D_MODEL, PROJ_W), D_MODEL ** -0.5),
        "mu_shift": jax.random.uniform(ks[12], (DEPTH, SHIFT_W), jnp.float32),
        "w0": jax.random.uniform(ks[13], (DEPTH, RWKV_W), jnp.float32, minval=-5.0, maxval=0.0),
        "w_lora2": nrm(ks[14], (DEPTH, DECAY_LORA, RWKV_W), 0.1),
        "a0": nrm(ks[15], (DEPTH, RWKV_W), 0.1),
        "a_lora2": nrm(ks[16], (DEPTH, AAA_LORA, RWKV_W), 0.5 * AAA_LORA ** -0.5),
        "g_lora2": nrm(ks[17], (DEPTH, GATE_LORA, RWKV_W), GATE_LORA ** -0.5),
        "k_k": 0.85 + nrm(ks[18], (DEPTH, RWKV_W), 0.05),
        "k_a": 1.0 + nrm(ks[19], (DEPTH, RWKV_W), 0.05),
        "r_k": nrm(ks[20], (DEPTH, RWKV_H, RWKV_HD), 0.1),
        "lnx_w": 1.0 + nrm(ks[21], (DEPTH, RWKV_W), 0.05),
        "lnx_b": nrm(ks[22], (DEPTH, RWKV_W), 0.01),
        "w_out": nrm(ks[23], (DEPTH, MIX_W, D_MODEL), MIX_W ** -0.5),
        "w_up": nrm(ks[24], (DEPTH, D_MODEL, D_FF), D_MODEL ** -0.5),
        "w_down": nrm(ks[25], (DEPTH, D_FF, D_MODEL), D_FF ** -0.5),
    }


def reference(x_prompt, x_sample, cache_k, cache_v, page_table, state_wkv, state_shift,
              g_mix_pre, g_mix_post, g_ffn_pre, g_ffn_post, w_in, mu_shift, w0, w_lora2,
              a0, a_lora2, g_lora2, k_k, k_a, r_k, lnx_w, lnx_b, w_out, w_up, w_down):
    pool_sums = jnp.sum(cache_k, axis=3, dtype=jnp.float32)
    yp, ys = x_prompt, x_sample
    b, s_len = yp.shape[0], yp.shape[1]
    kp_l, vp_l, wp_l, sp_l, ks_l, vs_l, ws_l, ss_l = [], [], [], [], [], [], [], []
    for l in range(DEPTH):
        rw = (mu_shift[l], w0[l], w_lora2[l], a0[l], a_lora2[l], g_lora2[l],
              k_k[l], k_a[l], r_k[l], lnx_w[l], lnx_b[l])
        mlp = (g_mix_pre[l], g_mix_post[l], g_ffn_pre[l], g_ffn_post[l],
               w_in[l], w_out[l], w_up[l], w_down[l])
        prompt_mixer = functools.partial(
            head_groups, rwkv_prev=jnp.zeros((b, SHIFT_W), yp.dtype),
            rwkv_s0=jnp.zeros((b, RWKV_H, RWKV_HD, RWKV_HD), jnp.float32),
            rwkv_params=rw, moba_fn=moba_prompt)
        yp, (k, v, s_fin, last) = trunk_layer(yp, prompt_mixer, *mlp)
        page_rows = lambda u: jnp.transpose(
            u.reshape(b, MOBA_H, s_len // PAGE_SIZE, PAGE_SIZE, MOBA_HD), (0, 2, 1, 3, 4))
        kp_l.append(page_rows(k)); vp_l.append(page_rows(v))
        wp_l.append(s_fin.astype(state_wkv.dtype)); sp_l.append(last.astype(state_shift.dtype))
        sample_moba = functools.partial(moba_sample, cache_k=cache_k, cache_v=cache_v,
                                        pool_sums=pool_sums[:, l], page_table=page_table, layer=l)
        sample_mixer = functools.partial(head_groups, rwkv_prev=state_shift[l], rwkv_s0=state_wkv[l],
                                         rwkv_params=rw, moba_fn=sample_moba)
        ys, (k, v, s_fin, last) = trunk_layer(ys, sample_mixer, *mlp)
        ks_l.append(k); vs_l.append(v)
        ws_l.append(s_fin.astype(state_wkv.dtype)); ss_l.append(last.astype(state_shift.dtype))
    k_prompt = jnp.stack(kp_l, axis=2)
    v_prompt = jnp.stack(vp_l, axis=2)
    wkv_prompt = jnp.stack(wp_l, axis=0)
    shift_prompt = jnp.stack(sp_l, axis=0)
    k_sample = jnp.stack(ks_l, axis=1)
    v_sample = jnp.stack(vs_l, axis=1)
    wkv_sample = jnp.stack(ws_l, axis=0)
    shift_sample = jnp.stack(ss_l, axis=0)
    return (yp, ys, k_prompt, v_prompt, wkv_prompt, shift_prompt,
            k_sample, v_sample, wkv_sample, shift_sample)
```

```python
import functools

import jax
import jax.numpy as jnp
from jax import lax
from jax.experimental import pallas as pl
from jax.experimental.pallas import tpu as pltpu

F32 = jnp.float32
BF16 = jnp.bfloat16

D_MODEL = 2048
RWKV_W = 1024
RWKV_HD = 64
RWKV_H = RWKV_W // RWKV_HD
DECAY_LORA = 64
AAA_LORA = 64
GATE_LORA = 160
MOBA_W = 1024
MOBA_HD = 128
MOBA_H = MOBA_W // MOBA_HD
MOBA_BLOCK = 256
MOBA_TOPK = 3
PAGE_SIZE = 128
D_FF = 4 * D_MODEL
RMS_EPS = 1e-6
GN_EPS = 64e-5
SHIFT_W = 3 * RWKV_W + DECAY_LORA + AAA_LORA + GATE_LORA

LANES = 128
VMEM_LIMIT_BYTES = 48 * 1024 * 1024

WD_OFF = 3 * RWKV_W
AD_OFF = WD_OFF + LANES
GD_OFF = AD_OFF + LANES
ZR_W = GD_OFF + 2 * LANES
PROJ_TN = 512
ZR_TILES = ZR_W // PROJ_TN
MOBA_TILES = MOBA_W // PROJ_TN
PROJ_PAD_W = ZR_W + 3 * MOBA_W
PROJ_TILES = PROJ_PAD_W // PROJ_TN

CHUNK = 64
RWKV_TB = 128
N_PAIRS = RWKV_W // LANES

NT = (((1,), (1,)), ((), ()))


def _cparams(*sem):
    return pltpu.CompilerParams(dimension_semantics=sem, vmem_limit_bytes=VMEM_LIMIT_BYTES)


def _mm(a, b):
    return jnp.dot(a.astype(BF16), b.astype(BF16), preferred_element_type=F32)


def _mm_nt(a, b):
    return lax.dot_general(a.astype(BF16), b.astype(BF16), NT, preferred_element_type=F32)


def _split2(x):
    hi = x.astype(BF16)
    lo = (x - hi.astype(F32)).astype(BF16)
    return hi, lo


def _mm3_nt(a, b):
    ah, al = _split2(a)
    bh, bl = _split2(b)
    d = lambda u, v: lax.dot_general(u, v, NT, preferred_element_type=F32)
    return d(ah, bh) + (d(ah, bl) + d(al, bh))


def _mm_exact_rhs(x, e):
    h0 = x.astype(BF16)
    r0 = x - h0.astype(F32)
    h1 = r0.astype(BF16)
    h2 = (r0 - h1.astype(F32)).astype(BF16)
    d = lambda u: jnp.dot(u, e, preferred_element_type=F32)
    return d(h0) + (d(h1) + d(h2))


def _mm_exact_lhs(e, x):
    h0 = x.astype(BF16)
    r0 = x - h0.astype(F32)
    h1 = r0.astype(BF16)
    h2 = (r0 - h1.astype(F32)).astype(BF16)
    d = lambda u: jnp.dot(e, u, preferred_element_type=F32)
    return d(h0) + (d(h1) + d(h2))


def _rms(x, g):
    ms = jnp.mean(x * x, axis=-1, keepdims=True)
    return (x * lax.rsqrt(ms + RMS_EPS)) * g


def _proj_kernel(x_ref, g_ref, w_ref, *refs, paged, tm):
    if paged:
        zr_ref, q_ref, kp_ref, vp_ref, kb_ref, vb_ref, xn_ref = refs
    else:
        zr_ref, q_ref, k_ref, v_ref, xn_ref = refs
    j = pl.program_id(1)

    @pl.when(j == 0)
    def _():
        xn_ref[...] = _rms(x_ref[...], g_ref[...]).astype(BF16)

    acc = jnp.dot(xn_ref[...], w_ref[...], preferred_element_type=F32)
    heads_per_tile = PROJ_TN // MOBA_HD
    q0, k0, v0 = ZR_TILES, ZR_TILES + MOBA_TILES, ZR_TILES + 2 * MOBA_TILES

    @pl.when(j < q0)
    def _():
        zr_ref[...] = acc

    if not paged:
        @pl.when((j >= q0) & (j < k0))
        def _():
            q_ref[...] = acc

        @pl.when((j >= k0) & (j < v0))
        def _():
            k_ref[...] = acc

        @pl.when(j >= v0)
        def _():
            v_ref[...] = acc
        return

    @pl.when((j >= q0) & (j < k0))
    def _():
        for hh in range(heads_per_tile):
            q_ref[hh] = acc[:, hh * MOBA_HD:(hh + 1) * MOBA_HD]

    def store_kv(page_ref, bf_ref):
        for hh in range(heads_per_tile):
            blk = acc[:, hh * MOBA_HD:(hh + 1) * MOBA_HD]
            bf_ref[hh] = blk.astype(BF16)
            for pg in range(tm // PAGE_SIZE):
                page_ref[pg, hh] = blk[pg * PAGE_SIZE:(pg + 1) * PAGE_SIZE]

    @pl.when((j >= k0) & (j < v0))
    def _():
        store_kv(kp_ref, kb_ref)

    @pl.when(j >= v0)
    def _():
        store_kv(vp_ref, vb_ref)


def _proj(x2, g, w_pad, *, paged):
    m = x2.shape[0]
    tm = 512 if paged else m
    assert m % tm == 0
    q0, k0, v0 = ZR_TILES, ZR_TILES + MOBA_TILES, ZR_TILES + 2 * MOBA_TILES
    hpt = PROJ_TN // MOBA_HD
    clip = lambda j, lo: jnp.clip(j - lo, 0, MOBA_TILES - 1)
    in_specs = [
        pl.BlockSpec((tm, D_MODEL), lambda i, j: (i, 0)),
        pl.BlockSpec((1, D_MODEL), lambda i, j: (0, 0)),
        pl.BlockSpec((D_MODEL, PROJ_TN), lambda i, j: (0, j)),
    ]
    zr_spec = pl.BlockSpec((tm, PROJ_TN), lambda i, j: (i, jnp.minimum(j, ZR_TILES - 1)))
    zr_shape = jax.ShapeDtypeStruct((m, ZR_W), F32)
    if paged:
        npg = m // PAGE_SIZE
        head_spec = lambda lo: pl.BlockSpec((hpt, tm, MOBA_HD), lambda i, j: (clip(j, lo), i, 0))
        page_spec = lambda lo: pl.BlockSpec((tm // PAGE_SIZE, hpt, PAGE_SIZE, MOBA_HD),
                                            lambda i, j: (i, clip(j, lo), 0, 0))
        out_specs = [zr_spec, head_spec(q0), page_spec(k0), page_spec(v0), head_spec(k0), head_spec(v0)]
        out_shape = [zr_shape,
                     jax.ShapeDtypeStruct((MOBA_H, m, MOBA_HD), F32),
                     jax.ShapeDtypeStruct((npg, MOBA_H, PAGE_SIZE, MOBA_HD), F32),
                     jax.ShapeDtypeStruct((npg, MOBA_H, PAGE_SIZE, MOBA_HD), F32),
                     jax.ShapeDtypeStruct((MOBA_H, m, MOBA_HD), BF16),
                     jax.ShapeDtypeStruct((MOBA_H, m, MOBA_HD), BF16)]
    else:
        flat_spec = lambda lo: pl.BlockSpec((tm, PROJ_TN), lambda i, j: (i, clip(j, lo)))
        out_specs = [zr_spec, flat_spec(q0), flat_spec(k0), flat_spec(v0)]
        out_shape = [zr_shape] + [jax.ShapeDtypeStruct((m, MOBA_W), F32)] * 3
    return pl.pallas_call(
        functools.partial(_proj_kernel, paged=paged, tm=tm),
        grid=(m // tm, PROJ_TILES),
        in_specs=in_specs, out_specs=out_specs, out_shape=out_shape,
        scratch_shapes=[pltpu.VMEM((tm, D_MODEL), BF16)],
        compiler_params=_cparams("parallel", "arbitrary"),
        name="proj_paged" if paged else "proj_flat",
    )(x2, g, w_pad)


def _segsum(x, e):
    outs = [_mm_exact_rhs(x[:, g * LANES:(g + 1) * LANES], e) for g in range(x.shape[1] // LANES)]
    return jnp.concatenate(outs, axis=1)


def _softplus(x):
    return jnp.maximum(x, 0.0) + jnp.log1p(jnp.exp(-jnp.abs(x)))


def _sigmoid(x):
    return 1.0 / (1.0 + jnp.exp(-x))


def _rwkv_pre(z, zprev, prm, e):
    mu, w0, a0, k_k, k_a, wl, al, gl = prm
    zs = z + mu * (zprev - z)
    r = zs[:, 0:RWKV_W]
    k = zs[:, RWKV_W:2 * RWKV_W]
    v = zs[:, 2 * RWKV_W:3 * RWKV_W]
    wd = zs[:, WD_OFF:AD_OFF]
    ad = zs[:, AD_OFF:GD_OFF]
    gd = zs[:, GD_OFF:ZR_W]
    w_log = -_softplus(-(w0 + _mm(jnp.tanh(wd), wl))) - 0.5
    lw = -jnp.exp(w_log)
    a = _sigmoid(a0 + _mm(ad, al))
    g = _mm(_sigmoid(gd), gl)
    kk = k * k_k
    kk = kk * lax.rsqrt(jnp.maximum(_segsum(kk * kk, e), 1e-24))
    k2 = k * (1.0 + (a - 1.0) * k_a)
    b = kk * a
    return r, lw, k2, v, kk, b, g


def _rwkv_post(o, r, k2, v, g, r_k, lnx_w, lnx_b, e):
    inv_hd = 1.0 / RWKV_HD
    mean = _segsum(o, e) * inv_hd
    d = o - mean
    var = _segsum(d * d, e) * inv_hd
    on = d * lax.rsqrt(var + GN_EPS) * lnx_w + lnx_b
    on = on + _segsum(r * k2 * r_k, e) * v
    return on * g


def _rwkv_prompt_kernel(z_ref, mu_ref, w0_ref, a0_ref, kk_ref, ka_ref, rk_ref, lw_ref, lb_ref,
                        wl_ref, al_ref, gl_ref, e_ref, tri_ref,
                        o_ref, s_out_ref,
                        carry_ref, s_ref, r_s, lw_s, k_s, v_s, kk_s, b_s, o_s):
    i = pl.program_id(0)
    tb = z_ref.shape[0]

    @pl.when(i == 0)
    def _():
        carry_ref[...] = jnp.zeros_like(carry_ref)
        s_ref[...] = jnp.zeros_like(s_ref)

    e = e_ref[...]
    z = z_ref[...]
    row = lax.broadcasted_iota(jnp.int32, (tb, 1), 0)
    zprev = jnp.where(row == 0, carry_ref[...], pltpu.roll(z, 1, axis=0))
    carry_ref[...] = z[tb - 1:tb, :]
    prm = (mu_ref[...], w0_ref[...], a0_ref[...], kk_ref[...], ka_ref[...],
           wl_ref[...], al_ref[...], gl_ref[...])
    r, lw, k2, v, kk, b, g = _rwkv_pre(z, zprev, prm, e)
    for p in range(N_PAIRS):
        sl = slice(p * LANES, (p + 1) * LANES)
        r_s[p] = r[:, sl]
        lw_s[p] = lw[:, sl]
        k_s[p] = k2[:, sl]
        v_s[p] = v[:, sl]
        kk_s[p] = kk[:, sl]
        b_s[p] = b[:, sl]

    lane = lax.broadcasted_iota(jnp.int32, (1, LANES), 1)
    m0 = (lane < RWKV_HD).astype(F32)
    m1 = 1.0 - m0
    ri = lax.broadcasted_iota(jnp.int32, (LANES, LANES), 0)
    ci = lax.broadcasted_iota(jnp.int32, (LANES, LANES), 1)
    rim, cim = ri & (CHUNK - 1), ci & (CHUNK - 1)
    strict = rim > cim
    incl = rim >= cim
    eye = (ri == ci).astype(F32)
    tri = tri_ref[...]

    def stack(x):
        return jnp.concatenate([x * m0, x * m1], axis=0)

    def chunk_pair(c, p):
        rows = pl.ds(c * CHUNK, CHUNK)
        r_ = r_s[p, rows, :]
        lw_ = lw_s[p, rows, :]
        k_ = k_s[p, rows, :]
        v_ = v_s[p, rows, :]
        kk_ = kk_s[p, rows, :]
        b_ = b_s[p, rows, :]
        s_bd = s_ref[p]
        cum = _mm_exact_lhs(tri, lw_)
        tot = cum[CHUNK - 1:CHUNK, :]
        gi = jnp.exp(-cum)
        gt = jnp.exp(tot - cum)
        a_st = stack(-kk_ * jnp.exp(cum - lw_))
        r_st = stack(r_ * jnp.exp(cum))
        b_st = stack(b_ * gi)
        k_st = stack(k_ * gi)
        bh_st = stack(b_ * gt)
        kh_st = stack(k_ * gt)
        v_st = stack(v_)
        ar = jnp.concatenate([a_st, r_st], axis=0)
        gm = _mm_nt(ar, jnp.concatenate([b_st, k_st], axis=0))
        n = jnp.where(strict, gm[:LANES, :LANES], 0.0)
        akm = jnp.where(strict, gm[:LANES, LANES:], 0.0)
        rbm = jnp.where(incl, gm[LANES:, :LANES], 0.0)
        rkm = jnp.where(incl, gm[LANES:, LANES:], 0.0)
        tinv = eye + n
        pw = n
        for _ in range(5):
            pw = _mm(pw, pw)
            tinv = tinv + _mm(tinv, pw)
        ah_rh = _mm_nt(ar, s_bd)
        w = ah_rh[:LANES] + _mm(akm, v_st)
        u_st = _mm(tinv, w)
        uv = jnp.concatenate([u_st, v_st], axis=0)
        o_st = ah_rh[LANES:] + _mm(jnp.concatenate([rbm, rkm], axis=1), uv)
        o_s[p, rows, :] = o_st[:CHUNK] + o_st[CHUNK:]
        uv_t = jnp.concatenate([u_st.T, v_st.T], axis=1)
        s_ref[p] = s_bd * jnp.exp(tot) + _mm(uv_t, jnp.concatenate([bh_st, kh_st], axis=0))

    for c in range(tb // CHUNK):
        def body(p, carry, c=c):
            chunk_pair(c, p)
            return carry
        lax.fori_loop(0, N_PAIRS, body, 0)

    o = jnp.concatenate([o_s[p] for p in range(N_PAIRS)], axis=1)
    out = _rwkv_post(o, r, k2, v, g, rk_ref[...], lw_ref[...], lb_ref[...], e)
    o_ref[...] = out.astype(o_ref.dtype)

    @pl.when(i == pl.num_programs(0) - 1)
    def _():
        s_out_ref[...] = s_ref[...]


def _full(shape):
    nd = len(shape)
    return pl.BlockSpec(shape, lambda *_: (0,) * nd)


def _rwkv_prompt(zr, rp):
    t = zr.shape[0]
    tb = RWKV_TB
    assert t % tb == 0 and tb % CHUNK == 0
    vec = lambda n: _full((1, n))
    in_specs = [pl.BlockSpec((tb, ZR_W), lambda i: (i, 0)),
                vec(ZR_W), vec(RWKV_W), vec(RWKV_W), vec(RWKV_W), vec(RWKV_W), vec(RWKV_W),
                vec(RWKV_W), vec(RWKV_W),
                _full((LANES, RWKV_W)), _full((LANES, RWKV_W)), _full((2 * LANES, RWKV_W)),
                _full((LANES, LANES)), _full((CHUNK, CHUNK))]
    pair_buf = pltpu.VMEM((N_PAIRS, tb, LANES), F32)
    return pl.pallas_call(
        _rwkv_prompt_kernel,
        grid=(t // tb,),
        in_specs=in_specs,
        out_specs=[pl.BlockSpec((tb, RWKV_W), lambda i: (i, 0)), _full((N_PAIRS, LANES, LANES))],
        out_shape=[jax.ShapeDtypeStruct((t, RWKV_W), BF16),
                   jax.ShapeDtypeStruct((N_PAIRS, LANES, LANES), F32)],
        scratch_shapes=[pltpu.VMEM((1, ZR_W), F32), pltpu.VMEM((N_PAIRS, LANES, LANES), F32)]
                       + [pair_buf] * 7,
        compiler_params=_cparams("arbitrary"),
        name="rwkv_prompt",
    )(zr, rp["mu"], rp["w0"], rp["a0"], rp["k_k"], rp["k_a"], rp["r_k"], rp["lnx_w"], rp["lnx_b"],
      rp["wl"], rp["al"], rp["gl"], rp["e"], rp["tri"])


def _rwkv_sample_pre_kernel(z_ref, prev_ref, mu_ref, w0_ref, a0_ref, kk_ref, ka_ref,
                            wl_ref, al_ref, gl_ref, e_ref,
                            r_o, w_o, k_o, v_o, kk_o, b_o, g_o):
    prm = (mu_ref[...], w0_ref[...], a0_ref[...], kk_ref[...], ka_ref[...],
           wl_ref[...], al_ref[...], gl_ref[...])
    r, lw, k2, v, kk, b, g = _rwkv_pre(z_ref[...], prev_ref[...], prm, e_ref[...])
    r_o[...] = r
    w_o[...] = jnp.exp(lw)
    k_o[...] = k2
    v_o[...] = v
    kk_o[...] = kk
    b_o[...] = b
    g_o[...] = g


def _rwkv_step_kernel(s_ref, r_ref, w_ref, k_ref, kk_ref, b_ref, v_ref, s_out, o_out):
    s = s_ref[0]
    sa = jnp.sum(s * (-kk_ref[0]), axis=-1, keepdims=True)
    s_new = s * w_ref[0] + sa * b_ref[0] + v_ref[0] * k_ref[0]
    s_out[0] = s_new
    o_out[0] = jnp.sum(s_new * r_ref[0], axis=-1, keepdims=True)


def _rwkv_sample_post_kernel(o_ref, r_ref, k_ref, v_ref, g_ref, rk_ref, lw_ref, lb_ref, e_ref, out_ref):
    out = _rwkv_post(o_ref[...], r_ref[...], k_ref[...], v_ref[...], g_ref[...],
                     rk_ref[...], lw_ref[...], lb_ref[...], e_ref[...])
    out_ref[...] = out.astype(out_ref.dtype)


def _rwkv_sample(zr, prev_pad, s0, rp):
    n = zr.shape[0]
    vec = lambda w: _full((1, w))
    rows = jax.ShapeDtypeStruct((n, RWKV_W), F32)
    r, w, k2, v, kk, b, g = pl.pallas_call(
        _rwkv_sample_pre_kernel,
        grid=(1,),
        in_specs=[_full((n, ZR_W)), _full((n, ZR_W)), vec(ZR_W), vec(RWKV_W), vec(RWKV_W), vec(RWKV_W),
                  vec(RWKV_W), _full((LANES, RWKV_W)), _full((LANES, RWKV_W)), _full((2 * LANES, RWKV_W)),
                  _full((LANES, LANES))],
        out_specs=[_full((n, RWKV_W))] * 7,
        out_shape=[rows] * 7,
        compiler_params=_cparams("arbitrary"),
        name="rwkv_sample_pre",
    )(zr, prev_pad, rp["mu"], rp["w0"], rp["a0"], rp["k_k"], rp["k_a"], rp["wl"], rp["al"], rp["gl"], rp["e"])

    as_row = lambda u: u.reshape(n, RWKV_H, 1, RWKV_HD)
    as_col = lambda u: u.reshape(n, RWKV_H, RWKV_HD, 1)
    row_spec = pl.BlockSpec((1, RWKV_H, 1, RWKV_HD), lambda i: (i, 0, 0, 0))
    col_spec = pl.BlockSpec((1, RWKV_H, RWKV_HD, 1), lambda i: (i, 0, 0, 0))
    st_spec = pl.BlockSpec((1, RWKV_H, RWKV_HD, RWKV_HD), lambda i: (i, 0, 0, 0))
    s_new, o = pl.pallas_call(
        _rwkv_step_kernel,
        grid=(n,),
        in_specs=[st_spec] + [row_spec] * 5 + [col_spec],
        out_specs=[st_spec, col_spec],
        out_shape=[jax.ShapeDtypeStruct((n, RWKV_H, RWKV_HD, RWKV_HD), F32),
                   jax.ShapeDtypeStruct((n, RWKV_H, RWKV_HD, 1), F32)],
        compiler_params=_cparams("parallel"),
        name="rwkv_sample_step",
    )(s0, as_row(r), as_row(w), as_row(k2), as_row(kk), as_row(b), as_col(v))

    out = pl.pallas_call(
        _rwkv_sample_post_kernel,
        grid=(1,),
        in_specs=[_full((n, RWKV_W))] * 5 + [vec(RWKV_W)] * 3 + [_full((LANES, LANES))],
        out_specs=_full((n, RWKV_W)),
        out_shape=jax.ShapeDtypeStruct((n, RWKV_W), BF16),
        compiler_params=_cparams("arbitrary"),
        name="rwkv_sample_post",
    )(o.reshape(n, RWKV_W), r, k2, v, g, rp["r_k"], rp["lnx_w"], rp["lnx_b"], rp["e"])
    return out, s_new


def _block_mean_kernel(k_ref, km_ref):
    x = k_ref[...]
    npg = x.shape[0]
    ppb = MOBA_BLOCK // PAGE_SIZE
    x = x.reshape(npg // ppb, MOBA_BLOCK, MOBA_HD)
    km_ref[0] = jnp.sum(x, axis=1) * (1.0 / MOBA_BLOCK)


def _block_means(k_pages):
    npg = k_pages.shape[0]
    nb = npg * PAGE_SIZE // MOBA_BLOCK
    return pl.pallas_call(
        _block_mean_kernel,
        grid=(MOBA_H,),
        in_specs=[pl.BlockSpec((npg, 1, PAGE_SIZE, MOBA_HD), lambda h: (0, h, 0, 0))],
        out_specs=pl.BlockSpec((1, nb, MOBA_HD), lambda h: (h, 0, 0)),
        out_shape=jax.ShapeDtypeStruct((MOBA_H, nb, MOBA_HD), F32),
        compiler_params=_cparams("parallel"),
        name="moba_block_means",
    )(k_pages)


def _top_blocks(gate, valid, col):
    neg = jnp.float32(-jnp.inf)
    g = jnp.where(valid, gate, neg)
    sel = jnp.zeros(gate.shape, jnp.bool_)
    big = jnp.int32(2 ** 30)
    for _ in range(MOBA_TOPK):
        mx = jnp.max(g, axis=1, keepdims=True)
        idx = jnp.min(jnp.where(g == mx, col, big), axis=1, keepdims=True)
        pick = (col == idx) & (mx > neg)
        sel = sel | pick
        g = jnp.where(pick, neg, g)
    return sel


def _moba_prompt_kernel(q_ref, km_ref, kb_ref, vb_ref, o_ref):
    qi = pl.program_id(1)
    tq = q_ref.shape[1]
    nb = km_ref.shape[1]
    scale = MOBA_HD ** -0.5
    neg = jnp.float32(-jnp.inf)
    q = q_ref[0]
    gate = _mm3_nt(q, km_ref[0])
    col = lax.broadcasted_iota(jnp.int32, (tq, nb), 1)
    sel = _top_blocks(gate, col < qi, col).astype(F32)
    qb = q.astype(BF16)

    start = pl.multiple_of(qi * MOBA_BLOCK, MOBA_BLOCK)
    kd = kb_ref[0, pl.ds(start, MOBA_BLOCK), :]
    vd = vb_ref[0, pl.ds(start, MOBA_BLOCK), :]
    s = lax.dot_general(qb, kd, NT, preferred_element_type=F32) * scale
    rr = lax.broadcasted_iota(jnp.int32, (tq, MOBA_BLOCK), 0)
    cc = lax.broadcasted_iota(jnp.int32, (tq, MOBA_BLOCK), 1)
    s = jnp.where(cc <= rr, s, neg)
    m = jnp.max(s, axis=1, keepdims=True)
    p = jnp.exp(s - m)
    l = jnp.sum(p, axis=1, keepdims=True)
    acc = jnp.dot(p.astype(BF16), vd, preferred_element_type=F32)

    def body(j, carry):
        m, l, acc = carry
        selcol = jnp.sum(jnp.where(col == j, sel, 0.0), axis=1, keepdims=True)
        st = pl.multiple_of(j * MOBA_BLOCK, MOBA_BLOCK)
        kj = kb_ref[0, pl.ds(st, MOBA_BLOCK), :]
        vj = vb_ref[0, pl.ds(st, MOBA_BLOCK), :]
        s = lax.dot_general(qb, kj, NT, preferred_element_type=F32) * scale
        s = jnp.where(selcol > 0.0, s, neg)
        m_new = jnp.maximum(m, jnp.max(s, axis=1, keepdims=True))
        alpha = jnp.exp(m - m_new)
        p = jnp.exp(s - m_new)
        l = alpha * l + jnp.sum(p, axis=1, keepdims=True)
        acc = alpha * acc + jnp.dot(p.astype(BF16), vj, preferred_element_type=F32)
        return m_new, l, acc

    m, l, acc = lax.fori_loop(0, qi, body, (m, l, acc))
    o_ref[...] = (acc / l).astype(o_ref.dtype)


def _moba_prompt(q3, km, kb, vb):
    t = q3.shape[1]
    nb = km.shape[1]
    assert t % MOBA_BLOCK == 0 and nb == t // MOBA_BLOCK
    return pl.pallas_call(
        _moba_prompt_kernel,
        grid=(MOBA_H, t // MOBA_BLOCK),
        in_specs=[pl.BlockSpec((1, MOBA_BLOCK, MOBA_HD), lambda h, i: (h, i, 0)),
                  pl.BlockSpec((1, nb, MOBA_HD), lambda h, i: (h, 0, 0)),
                  pl.BlockSpec((1, t, MOBA_HD), lambda h, i: (h, 0, 0)),
                  pl.BlockSpec((1, t, MOBA_HD), lambda h, i: (h, 0, 0))],
        out_specs=pl.BlockSpec((MOBA_BLOCK, MOBA_HD), lambda h, i: (i, h)),
        out_shape=jax.ShapeDtypeStruct((t, MOBA_W), BF16),
        compiler_params=_cparams("parallel", "arbitrary"),
        name="moba_prompt",
    )(q3, km, kb, vb)


PAGES_PER_STEP = 8


def _page_sum_kernel(pt_ref, *refs):
    del pt_ref
    k_refs, out_ref = refs[:PAGES_PER_STEP], refs[PAGES_PER_STEP]
    for h in range(MOBA_H):
        rows = [jnp.sum(k_refs[u][0, 0, h], axis=0, keepdims=True) for u in range(PAGES_PER_STEP)]
        out_ref[0, h] = jnp.concatenate(rows, axis=0)


def _page_sums(cache_k, page_table):
    nbt, n_pages = page_table.shape
    assert n_pages % PAGES_PER_STEP == 0
    spec = lambda u: pl.BlockSpec((1, 1, MOBA_H, PAGE_SIZE, MOBA_HD),
                                  lambda b, g, pt: (pt[b, g * PAGES_PER_STEP + u], 0, 0, 0, 0))
    return pl.pallas_call(
        _page_sum_kernel,
        grid_spec=pltpu.PrefetchScalarGridSpec(
            num_scalar_prefetch=1,
            grid=(nbt, n_pages // PAGES_PER_STEP),
            in_specs=[spec(u) for u in range(PAGES_PER_STEP)],
            out_specs=pl.BlockSpec((1, MOBA_H, PAGES_PER_STEP, MOBA_HD), lambda b, g, pt: (b, 0, g, 0))),
        out_shape=jax.ShapeDtypeStruct((nbt, MOBA_H, n_pages, MOBA_HD), F32),
        compiler_params=_cparams("parallel", "arbitrary"),
        name="moba_page_sums",
    )(page_table, *([cache_k] * PAGES_PER_STEP))


def _sample_gate_kernel(q_ref, ps_ref, sel_ref):
    n_pages = ps_ref.shape[2]
    lane = lax.broadcasted_iota(jnp.int32, (1, n_pages), 1)
    even = (lane & 1) == 0
    neg = jnp.float32(-jnp.inf)
    big = jnp.int32(2 ** 30)
    out_lane = lax.broadcasted_iota(jnp.int32, (1, LANES), 1)
    rows = []
    for h in range(MOBA_H):
        qh = jnp.broadcast_to(q_ref[0, h:h + 1, :], (8, MOBA_HD))
        gp = _mm3_nt(qh, ps_ref[0, h])[0:1, :]
        nbr = jnp.where(even, pltpu.roll(gp, n_pages - 1, axis=1), pltpu.roll(gp, 1, axis=1))
        g = jnp.where(even, (gp + nbr) * (1.0 / MOBA_BLOCK), neg)
        res = jnp.zeros((1, LANES), jnp.int32)
        for s in range(MOBA_TOPK):
            mx = jnp.max(g, axis=1, keepdims=True)
            idx = jnp.min(jnp.where(g == mx, lane, big), axis=1, keepdims=True)
            g = jnp.where(lane == idx, neg, g)
            res = jnp.where(out_lane == s, idx >> 1, res)
        rows.append(res)
    sel_ref[0] = jnp.concatenate(rows, axis=0)


def _sample_gate(q3, psums):
    nbt, _, n_pages, _ = psums.shape
    assert n_pages == LANES and n_pages // 2 >= MOBA_TOPK
    return pl.pallas_call(
        _sample_gate_kernel,
        grid=(nbt,),
        in_specs=[pl.BlockSpec((1, MOBA_H, MOBA_HD), lambda b: (b, 0, 0)),
                  pl.BlockSpec((1, MOBA_H, n_pages, MOBA_HD), lambda b: (b, 0, 0, 0))],
        out_specs=pl.BlockSpec((1, MOBA_H, LANES), lambda b: (b, 0, 0)),
        out_shape=jax.ShapeDtypeStruct((nbt, MOBA_H, LANES), jnp.int32),
        compiler_params=_cparams("parallel"),
        name="moba_sample_gate",
    )(q3, psums)


SAMPLE_PAGES = MOBA_TOPK * (MOBA_BLOCK // PAGE_SIZE)


def _sample_attn_kernel(ph_ref, q_ref, kn_ref, vn_ref, *refs):
    del ph_ref
    k_refs = refs[:SAMPLE_PAGES]
    v_refs = refs[SAMPLE_PAGES:2 * SAMPLE_PAGES]
    o_ref = refs[2 * SAMPLE_PAGES]
    scale = MOBA_HD ** -0.5
    q = q_ref[0, 0]
    s_self = jnp.sum(kn_ref[0, 0] * q, axis=1, keepdims=True) * scale
    ss = [jnp.sum(k_refs[u][0, 0, 0] * q, axis=1, keepdims=True) * scale for u in range(SAMPLE_PAGES)]
    m = s_self
    for s in ss:
        m = jnp.maximum(m, jnp.max(s, axis=0, keepdims=True))
    p_self = jnp.exp(s_self - m)
    l = p_self
    acc = p_self * vn_ref[0, 0]
    for u in range(SAMPLE_PAGES):
        p = jnp.exp(ss[u] - m)
        l = l + jnp.sum(p, axis=0, keepdims=True)
        acc = acc + jnp.sum(p * v_refs[u][0, 0, 0], axis=0, keepdims=True)
    o_ref[0, 0] = acc / l


def _sample_attn(q4, kn4, vn4, cache_k, cache_v, phys):
    nbt = q4.shape[0]
    vec_spec = pl.BlockSpec((1, 1, 1, MOBA_HD), lambda b, h, ph: (b, h, 0, 0))
    page_spec = lambda u: pl.BlockSpec(
        (1, 1, 1, PAGE_SIZE, MOBA_HD),
        lambda b, h, ph: (ph[(b * MOBA_H + h) * SAMPLE_PAGES + u], 0, h, 0, 0))
    pages = [page_spec(u) for u in range(SAMPLE_PAGES)]
    return pl.pallas_call(
        _sample_attn_kernel,
        grid_spec=pltpu.PrefetchScalarGridSpec(
            num_scalar_prefetch=1,
            grid=(nbt, MOBA_H),
            in_specs=[vec_spec, vec_spec, vec_spec] + pages + pages,
            out_specs=vec_spec),
        out_shape=jax.ShapeDtypeStruct((nbt, MOBA_H, 1, MOBA_HD), F32),
        compiler_params=_cparams("parallel", "arbitrary"),
        name="moba_sample_attn",
    )(phys, q4, kn4, vn4, *([cache_k] * SAMPLE_PAGES), *([cache_v] * SAMPLE_PAGES))


def _outproj_kernel(x_ref, or_ref, om_ref, wr_ref, wm_ref, g_ref, o_ref):
    y = (jnp.dot(or_ref[...], wr_ref[...], preferred_element_type=F32)
         + jnp.dot(om_ref[...], wm_ref[...], preferred_element_type=F32))
    o_ref[...] = x_ref[...] + _rms(y, g_ref[...])


def _outproj(x2, o_r, o_m, w_r, w_m, g):
    m = x2.shape[0]
    tm = min(m, 512)
    assert m % tm == 0
    return pl.pallas_call(
        _outproj_kernel,
        grid=(m // tm,),
        in_specs=[pl.BlockSpec((tm, D_MODEL), lambda i: (i, 0)),
                  pl.BlockSpec((tm, RWKV_W), lambda i: (i, 0)),
                  pl.BlockSpec((tm, MOBA_W), lambda i: (i, 0)),
                  _full((RWKV_W, D_MODEL)), _full((MOBA_W, D_MODEL)), _full((1, D_MODEL))],
        out_specs=pl.BlockSpec((tm, D_MODEL), lambda i: (i, 0)),
        out_shape=jax.ShapeDtypeStruct((m, D_MODEL), F32),
        compiler_params=_cparams("parallel"),
        name="outproj",
    )(x2, o_r, o_m, w_r, w_m, g)


MLP_TF = 1024


def _mlp_kernel(x_ref, g3_ref, g4_ref, wu_ref, wd_ref, o_ref, xn_ref, acc_ref):
    j = pl.program_id(1)

    @pl.when(j == 0)
    def _():
        xn_ref[...] = _rms(x_ref[...], g3_ref[...]).astype(BF16)
        acc_ref[...] = jnp.zeros_like(acc_ref)

    h = jnp.dot(xn_ref[...], wu_ref[...], preferred_element_type=F32)
    h = jnp.square(jnp.maximum(h, 0.0))
    acc_ref[...] += jnp.dot(h.astype(BF16), wd_ref[...], preferred_element_type=F32)

    @pl.when(j == pl.num_programs(1) - 1)
    def _():
        o_ref[...] = x_ref[...] + _rms(acc_ref[...], g4_ref[...])


def _mlp(x1, g3, g4, w_up, w_down):
    m = x1.shape[0]
    tm = min(m, 512)
    assert m % tm == 0
    return pl.pallas_call(
        _mlp_kernel,
        grid=(m // tm, D_FF // MLP_TF),
        in_specs=[pl.BlockSpec((tm, D_MODEL), lambda i, j: (i, 0)),
                  pl.BlockSpec((1, D_MODEL), lambda i, j: (0, 0)),
                  pl.BlockSpec((1, D_MODEL), lambda i, j: (0, 0)),
                  pl.BlockSpec((D_MODEL, MLP_TF), lambda i, j: (0, j)),
                  pl.BlockSpec((MLP_TF, D_MODEL), lambda i, j: (j, 0))],
        out_specs=pl.BlockSpec((tm, D_MODEL), lambda i, j: (i, 0)),
        out_shape=jax.ShapeDtypeStruct((m, D_MODEL), F32),
        scratch_shapes=[pltpu.VMEM((tm, D_MODEL), BF16), pltpu.VMEM((tm, D_MODEL), F32)],
        compiler_params=_cparams("parallel", "arbitrary"),
        name="mlp",
    )(x1, g3, g4, w_up, w_down)


def _pad_cols(a):
    z = lambda n: jnp.zeros(a.shape[:-1] + (n,), a.dtype)
    wd = a[..., 3 * RWKV_W:3 * RWKV_W + DECAY_LORA]
    ad = a[..., 3 * RWKV_W + DECAY_LORA:3 * RWKV_W + DECAY_LORA + AAA_LORA]
    gd = a[..., 3 * RWKV_W + DECAY_LORA + AAA_LORA:SHIFT_W]
    parts = [a[..., :3 * RWKV_W], wd, z(LANES - DECAY_LORA), ad, z(LANES - AAA_LORA),
             gd, z(2 * LANES - GATE_LORA)]
    if a.shape[-1] > SHIFT_W:
        parts.append(a[..., SHIFT_W:])
    return jnp.concatenate(parts, axis=-1)


def _unpad_cols(zr):
    return jnp.concatenate([zr[..., :3 * RWKV_W + DECAY_LORA], zr[..., AD_OFF:AD_OFF + AAA_LORA],
                            zr[..., GD_OFF:GD_OFF + GATE_LORA]], axis=-1)


def _pad_rows(a, n):
    return jnp.concatenate([a, jnp.zeros((n - a.shape[0],) + a.shape[1:], a.dtype)], axis=0)


def _unpair_state(s_bd):
    h0 = s_bd[:, :RWKV_HD, :RWKV_HD]
    h1 = s_bd[:, RWKV_HD:, RWKV_HD:]
    return jnp.stack([h0, h1], axis=1).reshape(RWKV_H, RWKV_HD, RWKV_HD)


def kernel(x_prompt, x_sample, cache_k, cache_v, page_table, state_wkv, state_shift, g_mix_pre, g_mix_post, g_ffn_pre, g_ffn_post, w_in, mu_shift, w0, w_lora2, a0, a_lora2, g_lora2, k_k, k_a, r_k, lnx_w, lnx_b, w_out, w_up, w_down):
    depth = w_in.shape[0]
    bsz, seq, _ = x_prompt.shape
    dbt, dseq, _ = x_sample.shape
    assert depth == 1 and bsz == 1 and dseq == 1, "kernels are written for the stated shapes"
    n_pages = page_table.shape[1]
    l = 0

    row = lambda a: a.reshape(1, -1)
    w_pad = _pad_cols(w_in[l]).astype(BF16)
    lane = jnp.arange(LANES)
    e_seg = ((lane[:, None] // RWKV_HD) == (lane[None, :] // RWKV_HD)).astype(BF16)
    ci = jnp.arange(CHUNK)
    tri = (ci[:, None] >= ci[None, :]).astype(BF16)
    rp = dict(mu=row(_pad_cols(mu_shift[l])), w0=row(w0[l]), a0=row(a0[l]), k_k=row(k_k[l]),
              k_a=row(k_a[l]), r_k=row(r_k[l]), lnx_w=row(lnx_w[l]), lnx_b=row(lnx_b[l]),
              wl=_pad_rows(w_lora2[l], LANES).astype(BF16), al=_pad_rows(a_lora2[l], LANES).astype(BF16),
              gl=_pad_rows(g_lora2[l], 2 * LANES).astype(BF16), e=e_seg, tri=tri)
    w_or = w_out[l, :RWKV_W].astype(BF16)
    w_om = w_out[l, RWKV_W:].astype(BF16)
    w_up_b = w_up[l].astype(BF16)
    w_down_b = w_down[l].astype(BF16)
    g1, g2, g3, g4 = row(g_mix_pre[l]), row(g_mix_post[l]), row(g_ffn_pre[l]), row(g_ffn_post[l])

    xp = x_prompt.reshape(seq, D_MODEL)
    zr, q3, k_pages, v_pages, kb, vb = _proj(xp, g1, w_pad, paged=True)
    o_r, s_bd = _rwkv_prompt(zr, rp)
    km = _block_means(k_pages)
    o_m = _moba_prompt(q3, km, kb, vb)
    x1 = _outproj(xp, o_r, o_m, w_or, w_om, g2)
    y_prompt = _mlp(x1, g3, g4, w_up_b, w_down_b).reshape(bsz, seq, D_MODEL)
    npg = seq // PAGE_SIZE
    k_prompt = k_pages.reshape(bsz, npg, 1, MOBA_H, PAGE_SIZE, MOBA_HD)
    v_prompt = v_pages.reshape(bsz, npg, 1, MOBA_H, PAGE_SIZE, MOBA_HD)
    wkv_prompt = _unpair_state(s_bd).reshape(1, bsz, RWKV_H, RWKV_HD, RWKV_HD)
    shift_prompt = _unpad_cols(zr[seq - 1:seq]).reshape(1, bsz, SHIFT_W)

    xs = x_sample.reshape(dbt, D_MODEL)
    zr_s, q_s, k_s, v_s = _proj(xs, g1, w_pad, paged=False)
    o_rs, wkv_s = _rwkv_sample(zr_s, _pad_cols(state_shift[l]), state_wkv[l], rp)
    psums = _page_sums(cache_k, page_table)
    sel = _sample_gate(q_s.reshape(dbt, MOBA_H, MOBA_HD), psums)[:, :, :MOBA_TOPK]
    ppb = MOBA_BLOCK // PAGE_SIZE
    logical = sel[..., None] * ppb + jnp.arange(ppb, dtype=jnp.int32)
    phys = jnp.take_along_axis(page_table, logical.reshape(dbt, -1), axis=1).reshape(-1)
    as4 = lambda u: u.reshape(dbt, MOBA_H, 1, MOBA_HD)
    o_ms = _sample_attn(as4(q_s), as4(k_s), as4(v_s), cache_k, cache_v, phys)
    o_ms = o_ms.reshape(dbt, MOBA_W).astype(BF16)
    x1s = _outproj(xs, o_rs, o_ms, w_or, w_om, g2)
    y_sample = _mlp(x1s, g3, g4, w_up_b, w_down_b).reshape(dbt, dseq, D_MODEL)
    k_sample = k_s.reshape(dbt, 1, MOBA_H, dseq, MOBA_HD)
    v_sample = v_s.reshape(dbt, 1, MOBA_H, dseq, MOBA_HD)
    wkv_sample = wkv_s.reshape(1, dbt, RWKV_H, RWKV_HD, RWKV_HD)
    shift_sample = _unpad_cols(zr_s).reshape(1, dbt, SHIFT_W)

    return (y_prompt, y_sample, k_prompt, v_prompt, wkv_prompt, shift_prompt,
            k_sample, v_sample, wkv_sample, shift_sample)
```

```python
import functools

import jax
import jax.numpy as jnp
from jax import lax
from jax.experimental import pallas as pl
from jax.experimental.pallas import tpu as pltpu

F32 = jnp.float32
BF16 = jnp.bfloat16

D_MODEL = 2048
RWKV_W = 1024
RWKV_HD = 64
RWKV_H = RWKV_W // RWKV_HD
DECAY_LORA = 64
AAA_LORA = 64
GATE_LORA = 160
MOBA_W = 1024
MOBA_HD = 128
MOBA_H = MOBA_W // MOBA_HD
MOBA_BLOCK = 256
MOBA_TOPK = 3
PAGE_SIZE = 128
D_FF = 4 * D_MODEL
RMS_EPS = 1e-6
GN_EPS = 64e-5
SHIFT_W = 3 * RWKV_W + DECAY_LORA + AAA_LORA + GATE_LORA

LANES = 128
VMEM_LIMIT_BYTES = 48 * 1024 * 1024

WD_OFF = 3 * RWKV_W
AD_OFF = WD_OFF + LANES
GD_OFF = AD_OFF + LANES
ZR_W = GD_OFF + 2 * LANES
PROJ_TN = 512
ZR_TILES = ZR_W // PROJ_TN
MOBA_TILES = MOBA_W // PROJ_TN
PROJ_PAD_W = ZR_W + 3 * MOBA_W
PROJ_TILES = PROJ_PAD_W // PROJ_TN

CHUNK = 64
RWKV_TB = 256
N_PAIRS = RWKV_W // LANES

NT = (((1,), (1,)), ((), ()))


def _cparams(*sem):
    return pltpu.CompilerParams(dimension_semantics=sem, vmem_limit_bytes=VMEM_LIMIT_BYTES)


def _mm(a, b):
    return jnp.dot(a.astype(BF16), b.astype(BF16), preferred_element_type=F32)


def _mm_nt(a, b):
    return lax.dot_general(a.astype(BF16), b.astype(BF16), NT, preferred_element_type=F32)


def _split2(x):
    hi = x.astype(BF16)
    lo = (x - hi.astype(F32)).astype(BF16)
    return hi, lo


def _mm3_nt(a, b):
    ah, al = _split2(a)
    bh, bl = _split2(b)
    d = lambda u, v: lax.dot_general(u, v, NT, preferred_element_type=F32)
    return d(ah, bh) + (d(ah, bl) + d(al, bh))


def _mm_exact_rhs(x, e):
    h0 = x.astype(BF16)
    r0 = x - h0.astype(F32)
    h1 = r0.astype(BF16)
    h2 = (r0 - h1.astype(F32)).astype(BF16)
    d = lambda u: jnp.dot(u, e, preferred_element_type=F32)
    return d(h0) + (d(h1) + d(h2))


def _mm_exact_lhs(e, x):
    h0 = x.astype(BF16)
    r0 = x - h0.astype(F32)
    h1 = r0.astype(BF16)
    h2 = (r0 - h1.astype(F32)).astype(BF16)
    d = lambda u: jnp.dot(e, u, preferred_element_type=F32)
    return d(h0) + (d(h1) + d(h2))


def _rms(x, g):
    ms = jnp.mean(x * x, axis=-1, keepdims=True)
    return (x * lax.rsqrt(ms + RMS_EPS)) * g


def _proj_kernel(x_ref, g_ref, w_ref, *refs, paged, tm):
    if paged:
        zr_ref, q_ref, kp_ref, vp_ref, kb_ref, vb_ref, xn_ref = refs
    else:
        zr_ref, q_ref, k_ref, v_ref, xn_ref = refs
    j = pl.program_id(1)

    @pl.when(j == 0)
    def _():
        xn_ref[...] = _rms(x_ref[...], g_ref[...]).astype(BF16)

    acc = jnp.dot(xn_ref[...], w_ref[...], preferred_element_type=F32)
    heads_per_tile = PROJ_TN // MOBA_HD
    q0, k0, v0 = ZR_TILES, ZR_TILES + MOBA_TILES, ZR_TILES + 2 * MOBA_TILES

    @pl.when(j < q0)
    def _():
        zr_ref[...] = acc

    if not paged:
        @pl.when((j >= q0) & (j < k0))
        def _():
            q_ref[...] = acc

        @pl.when((j >= k0) & (j < v0))
        def _():
            k_ref[...] = acc

        @pl.when(j >= v0)
        def _():
            v_ref[...] = acc
        return

    @pl.when((j >= q0) & (j < k0))
    def _():
        for hh in range(heads_per_tile):
            q_ref[hh] = acc[:, hh * MOBA_HD:(hh + 1) * MOBA_HD]

    def store_kv(page_ref, bf_ref):
        for hh in range(heads_per_tile):
            blk = acc[:, hh * MOBA_HD:(hh + 1) * MOBA_HD]
            bf_ref[hh] = blk.astype(BF16)
            for pg in range(tm // PAGE_SIZE):
                page_ref[pg, hh] = blk[pg * PAGE_SIZE:(pg + 1) * PAGE_SIZE]

    @pl.when((j >= k0) & (j < v0))
    def _():
        store_kv(kp_ref, kb_ref)

    @pl.when(j >= v0)
    def _():
        store_kv(vp_ref, vb_ref)


def _proj(x2, g, w_pad, *, paged):
    m = x2.shape[0]
    tm = 1024 if paged else m
    assert m % tm == 0
    q0, k0, v0 = ZR_TILES, ZR_TILES + MOBA_TILES, ZR_TILES + 2 * MOBA_TILES
    hpt = PROJ_TN // MOBA_HD
    clip = lambda j, lo: jnp.clip(j - lo, 0, MOBA_TILES - 1)
    in_specs = [
        pl.BlockSpec((tm, D_MODEL), lambda i, j: (i, 0)),
        pl.BlockSpec((1, D_MODEL), lambda i, j: (0, 0)),
        pl.BlockSpec((D_MODEL, PROJ_TN), lambda i, j: (0, j)),
    ]
    zr_spec = pl.BlockSpec((tm, PROJ_TN), lambda i, j: (i, jnp.minimum(j, ZR_TILES - 1)))
    zr_shape = jax.ShapeDtypeStruct((m, ZR_W), F32)
    if paged:
        npg = m // PAGE_SIZE
        head_spec = lambda lo: pl.BlockSpec((hpt, tm, MOBA_HD), lambda i, j: (clip(j, lo), i, 0))
        page_spec = lambda lo: pl.BlockSpec((tm // PAGE_SIZE, hpt, PAGE_SIZE, MOBA_HD),
                                            lambda i, j: (i, clip(j, lo), 0, 0))
        out_specs = [zr_spec, head_spec(q0), page_spec(k0), page_spec(v0), head_spec(k0), head_spec(v0)]
        out_shape = [zr_shape,
                     jax.ShapeDtypeStruct((MOBA_H, m, MOBA_HD), F32),
                     jax.ShapeDtypeStruct((npg, MOBA_H, PAGE_SIZE, MOBA_HD), F32),
                     jax.ShapeDtypeStruct((npg, MOBA_H, PAGE_SIZE, MOBA_HD), F32),
                     jax.ShapeDtypeStruct((MOBA_H, m, MOBA_HD), BF16),
                     jax.ShapeDtypeStruct((MOBA_H, m, MOBA_HD), BF16)]
    else:
        flat_spec = lambda lo: pl.BlockSpec((tm, PROJ_TN), lambda i, j: (i, clip(j, lo)))
        out_specs = [zr_spec, flat_spec(q0), flat_spec(k0), flat_spec(v0)]
        out_shape = [zr_shape] + [jax.ShapeDtypeStruct((m, MOBA_W), F32)] * 3
    return pl.pallas_call(
        functools.partial(_proj_kernel, paged=paged, tm=tm),
        grid=(m // tm, PROJ_TILES),
        in_specs=in_specs, out_specs=out_specs, out_shape=out_shape,
        scratch_shapes=[pltpu.VMEM((tm, D_MODEL), BF16)],
        compiler_params=_cparams("parallel", "arbitrary"),
        name="proj_paged" if paged else "proj_flat",
    )(x2, g, w_pad)


def _segsum(x, e):
    outs = [_mm_exact_rhs(x[:, g * LANES:(g + 1) * LANES], e) for g in range(x.shape[1] // LANES)]
    return jnp.concatenate(outs, axis=1)


def _softplus(x):
    return jnp.maximum(x, 0.0) + jnp.log1p(jnp.exp(-jnp.abs(x)))


def _sigmoid(x):
    return 1.0 / (1.0 + jnp.exp(-x))


def _rwkv_pre(z, zprev, prm, e):
    mu, w0, a0, k_k, k_a, wl, al, gl = prm
    zs = z + mu * (zprev - z)
    r = zs[:, 0:RWKV_W]
    k = zs[:, RWKV_W:2 * RWKV_W]
    v = zs[:, 2 * RWKV_W:3 * RWKV_W]
    wd = zs[:, WD_OFF:AD_OFF]
    ad = zs[:, AD_OFF:GD_OFF]
    gd = zs[:, GD_OFF:ZR_W]
    w_log = -_softplus(-(w0 + _mm(jnp.tanh(wd), wl))) - 0.5
    lw = -jnp.exp(w_log)
    a = _sigmoid(a0 + _mm(ad, al))
    g = _mm(_sigmoid(gd), gl)
    kk = k * k_k
    kk = kk * lax.rsqrt(jnp.maximum(_segsum(kk * kk, e), 1e-24))
    k2 = k * (1.0 + (a - 1.0) * k_a)
    b = kk * a
    return r, lw, k2, v, kk, b, g


def _rwkv_post(o, r, k2, v, g, r_k, lnx_w, lnx_b, e):
    inv_hd = 1.0 / RWKV_HD
    mean = _segsum(o, e) * inv_hd
    d = o - mean
    var = _segsum(d * d, e) * inv_hd
    on = d * lax.rsqrt(var + GN_EPS) * lnx_w + lnx_b
    on = on + _segsum(r * k2 * r_k, e) * v
    return on * g


def _rwkv_prompt_kernel(z_ref, mu_ref, w0_ref, a0_ref, kk_ref, ka_ref, rk_ref, lw_ref, lb_ref,
                        wl_ref, al_ref, gl_ref, e_ref, tri_ref,
                        o_ref, s_out_ref,
                        carry_ref, s_ref, r_s, lw_s, k_s, v_s, kk_s, b_s, o_s):
    i = pl.program_id(0)
    tb = z_ref.shape[0]

    @pl.when(i == 0)
    def _():
        carry_ref[...] = jnp.zeros_like(carry_ref)
        s_ref[...] = jnp.zeros_like(s_ref)

    e = e_ref[...]
    z = z_ref[...]
    row = lax.broadcasted_iota(jnp.int32, (tb, 1), 0)
    zprev = jnp.where(row == 0, carry_ref[...], pltpu.roll(z, 1, axis=0))
    carry_ref[...] = z[tb - 1:tb, :]
    prm = (mu_ref[...], w0_ref[...], a0_ref[...], kk_ref[...], ka_ref[...],
           wl_ref[...], al_ref[...], gl_ref[...])
    r, lw, k2, v, kk, b, g = _rwkv_pre(z, zprev, prm, e)
    for p in range(N_PAIRS):
        sl = slice(p * LANES, (p + 1) * LANES)
        r_s[p] = r[:, sl]
        lw_s[p] = lw[:, sl]
        k_s[p] = k2[:, sl]
        v_s[p] = v[:, sl]
        kk_s[p] = kk[:, sl]
        b_s[p] = b[:, sl]

    lane = lax.broadcasted_iota(jnp.int32, (1, LANES), 1)
    m0 = (lane < RWKV_HD).astype(F32)
    m1 = 1.0 - m0
    ri = lax.broadcasted_iota(jnp.int32, (LANES, LANES), 0)
    ci = lax.broadcasted_iota(jnp.int32, (LANES, LANES), 1)
    rim, cim = ri & (CHUNK - 1), ci & (CHUNK - 1)
    strict = rim > cim
    incl = rim >= cim
    eye = (ri == ci).astype(F32)
    tri = tri_ref[...]

    def stack(x):
        return jnp.concatenate([x * m0, x * m1], axis=0)

    pairs = range(N_PAIRS)

    def chunk_body(c, carry):
        rows = pl.ds(pl.multiple_of(c * CHUNK, CHUNK), CHUNK)
        cum = [_mm_exact_lhs(tri, lw_s[p, rows, :]) for p in pairs]
        ar, bk, bkh, v_st, dec = [], [], [], [], []
        for p in pairs:
            lw_ = lw_s[p, rows, :]
            k_ = k_s[p, rows, :]
            b_ = b_s[p, rows, :]
            tot = cum[p][CHUNK - 1:CHUNK, :]
            gi = jnp.exp(-cum[p])
            gt = jnp.exp(tot - cum[p])
            a_st = stack(-kk_s[p, rows, :] * jnp.exp(cum[p] - lw_))
            r_st = stack(r_s[p, rows, :] * jnp.exp(cum[p]))
            ar.append(jnp.concatenate([a_st, r_st], axis=0).astype(BF16))
            bk.append(jnp.concatenate([stack(b_ * gi), stack(k_ * gi)], axis=0).astype(BF16))
            bkh.append(jnp.concatenate([stack(b_ * gt), stack(k_ * gt)], axis=0).astype(BF16))
            v_st.append(stack(v_s[p, rows, :]))
            dec.append(jnp.exp(tot))
        gm = [_mm_nt(ar[p], bk[p]) for p in pairs]
        ah_rh = [_mm_nt(ar[p], s_ref[p]) for p in pairs]
        n = [jnp.where(strict, gm[p][:LANES, :LANES], 0.0) for p in pairs]
        akm = [jnp.where(strict, gm[p][:LANES, LANES:], 0.0) for p in pairs]
        rbk = [jnp.concatenate([jnp.where(incl, gm[p][LANES:, :LANES], 0.0),
                                jnp.where(incl, gm[p][LANES:, LANES:], 0.0)], axis=1).astype(BF16)
               for p in pairs]
        w = [_mm(akm[p], v_st[p]) for p in pairs]
        tinv = [eye + n[p] for p in pairs]
        pw = n
        for _ in range(5):
            pw = [_mm(pw[p], pw[p]) for p in pairs]
            tinv = [tinv[p] + _mm(tinv[p], pw[p]) for p in pairs]
        u_st = [_mm(tinv[p], ah_rh[p][:LANES] + w[p]) for p in pairs]
        uv = [jnp.concatenate([u_st[p], v_st[p]], axis=0).astype(BF16) for p in pairs]
        o_st = [ah_rh[p][LANES:] + _mm(rbk[p], uv[p]) for p in pairs]
        for p in pairs:
            o_s[p, rows, :] = o_st[p][:CHUNK] + o_st[p][CHUNK:]
        uv_t = [jnp.concatenate([u_st[p].T, v_st[p].T], axis=1) for p in pairs]
        s_new = [s_ref[p] * dec[p] + _mm(uv_t[p], bkh[p]) for p in pairs]
        for p in pairs:
            s_ref[p] = s_new[p]
        return carry

    lax.fori_loop(0, tb // CHUNK, chunk_body, 0)

    o = jnp.concatenate([o_s[p] for p in range(N_PAIRS)], axis=1)
    out = _rwkv_post(o, r, k2, v, g, rk_ref[...], lw_ref[...], lb_ref[...], e)
    o_ref[...] = out.astype(o_ref.dtype)

    @pl.when(i == pl.num_programs(0) - 1)
    def _():
        s_out_ref[...] = s_ref[...]


def _full(shape):
    nd = len(shape)
    return pl.BlockSpec(shape, lambda *_: (0,) * nd)


def _rwkv_prompt(zr, rp):
    t = zr.shape[0]
    tb = RWKV_TB
    assert t % tb == 0 and tb % CHUNK == 0
    vec = lambda n: _full((1, n))
    in_specs = [pl.BlockSpec((tb, ZR_W), lambda i: (i, 0)),
                vec(ZR_W), vec(RWKV_W), vec(RWKV_W), vec(RWKV_W), vec(RWKV_W), vec(RWKV_W),
                vec(RWKV_W), vec(RWKV_W),
                _full((LANES, RWKV_W)), _full((LANES, RWKV_W)), _full((2 * LANES, RWKV_W)),
                _full((LANES, LANES)), _full((CHUNK, CHUNK))]
    pair_buf = pltpu.VMEM((N_PAIRS, tb, LANES), F32)
    return pl.pallas_call(
        _rwkv_prompt_kernel,
        grid=(t // tb,),
        in_specs=in_specs,
        out_specs=[pl.BlockSpec((tb, RWKV_W), lambda i: (i, 0)), _full((N_PAIRS, LANES, LANES))],
        out_shape=[jax.ShapeDtypeStruct((t, RWKV_W), BF16),
                   jax.ShapeDtypeStruct((N_PAIRS, LANES, LANES), F32)],
        scratch_shapes=[pltpu.VMEM((1, ZR_W), F32), pltpu.VMEM((N_PAIRS, LANES, LANES), F32)]
                       + [pair_buf] * 7,
        compiler_params=_cparams("arbitrary"),
        name="rwkv_prompt",
    )(zr, rp["mu"], rp["w0"], rp["a0"], rp["k_k"], rp["k_a"], rp["r_k"], rp["lnx_w"], rp["lnx_b"],
      rp["wl"], rp["al"], rp["gl"], rp["e"], rp["tri"])


def _rwkv_sample_pre_kernel(z_ref, prev_ref, mu_ref, w0_ref, a0_ref, kk_ref, ka_ref,
                            wl_ref, al_ref, gl_ref, e_ref,
                            r_o, w_o, k_o, v_o, kk_o, b_o, g_o):
    prm = (mu_ref[...], w0_ref[...], a0_ref[...], kk_ref[...], ka_ref[...],
           wl_ref[...], al_ref[...], gl_ref[...])
    r, lw, k2, v, kk, b, g = _rwkv_pre(z_ref[...], prev_ref[...], prm, e_ref[...])
    r_o[...] = r
    w_o[...] = jnp.exp(lw)
    k_o[...] = k2
    v_o[...] = v
    kk_o[...] = kk
    b_o[...] = b
    g_o[...] = g


def _rwkv_step_kernel(s_ref, r_ref, w_ref, k_ref, kk_ref, b_ref, v_ref, s_out, o_out):
    s = s_ref[0]
    sa = jnp.sum(s * (-kk_ref[0]), axis=-1, keepdims=True)
    s_new = s * w_ref[0] + sa * b_ref[0] + v_ref[0] * k_ref[0]
    s_out[0] = s_new
    o_out[0] = jnp.sum(s_new * r_ref[0], axis=-1, keepdims=True)


def _rwkv_sample_post_kernel(o_ref, r_ref, k_ref, v_ref, g_ref, rk_ref, lw_ref, lb_ref, e_ref, out_ref):
    out = _rwkv_post(o_ref[...], r_ref[...], k_ref[...], v_ref[...], g_ref[...],
                     rk_ref[...], lw_ref[...], lb_ref[...], e_ref[...])
    out_ref[...] = out.astype(out_ref.dtype)


def _rwkv_sample(zr, prev_pad, s0, rp):
    n = zr.shape[0]
    vec = lambda w: _full((1, w))
    rows = jax.ShapeDtypeStruct((n, RWKV_W), F32)
    r, w, k2, v, kk, b, g = pl.pallas_call(
        _rwkv_sample_pre_kernel,
        grid=(1,),
        in_specs=[_full((n, ZR_W)), _full((n, ZR_W)), vec(ZR_W), vec(RWKV_W), vec(RWKV_W), vec(RWKV_W),
                  vec(RWKV_W), _full((LANES, RWKV_W)), _full((LANES, RWKV_W)), _full((2 * LANES, RWKV_W)),
                  _full((LANES, LANES))],
        out_specs=[_full((n, RWKV_W))] * 7,
        out_shape=[rows] * 7,
        compiler_params=_cparams("arbitrary"),
        name="rwkv_sample_pre",
    )(zr, prev_pad, rp["mu"], rp["w0"], rp["a0"], rp["k_k"], rp["k_a"], rp["wl"], rp["al"], rp["gl"], rp["e"])

    as_row = lambda u: u.reshape(n, RWKV_H, 1, RWKV_HD)
    as_col = lambda u: u.reshape(n, RWKV_H, RWKV_HD, 1)
    row_spec = pl.BlockSpec((1, RWKV_H, 1, RWKV_HD), lambda i: (i, 0, 0, 0))
    col_spec = pl.BlockSpec((1, RWKV_H, RWKV_HD, 1), lambda i: (i, 0, 0, 0))
    st_spec = pl.BlockSpec((1, RWKV_H, RWKV_HD, RWKV_HD), lambda i: (i, 0, 0, 0))
    s_new, o = pl.pallas_call(
        _rwkv_step_kernel,
        grid=(n,),
        in_specs=[st_spec] + [row_spec] * 5 + [col_spec],
        out_specs=[st_spec, col_spec],
        out_shape=[jax.ShapeDtypeStruct((n, RWKV_H, RWKV_HD, RWKV_HD), F32),
                   jax.ShapeDtypeStruct((n, RWKV_H, RWKV_HD, 1), F32)],
        compiler_params=_cparams("parallel"),
        name="rwkv_sample_step",
    )(s0, as_row(r), as_row(w), as_row(k2), as_row(kk), as_row(b), as_col(v))

    out = pl.pallas_call(
        _rwkv_sample_post_kernel,
        grid=(1,),
        in_specs=[_full((n, RWKV_W))] * 5 + [vec(RWKV_W)] * 3 + [_full((LANES, LANES))],
        out_specs=_full((n, RWKV_W)),
        out_shape=jax.ShapeDtypeStruct((n, RWKV_W), BF16),
        compiler_params=_cparams("arbitrary"),
        name="rwkv_sample_post",
    )(o.reshape(n, RWKV_W), r, k2, v, g, rp["r_k"], rp["lnx_w"], rp["lnx_b"], rp["e"])
    return out, s_new


def _block_mean_kernel(k_ref, km_ref):
    x = k_ref[...]
    npg = x.shape[0]
    ppb = MOBA_BLOCK // PAGE_SIZE
    x = x.reshape(npg // ppb, MOBA_BLOCK, MOBA_HD)
    km_ref[0] = jnp.sum(x, axis=1) * (1.0 / MOBA_BLOCK)


def _block_means(k_pages):
    npg = k_pages.shape[0]
    nb = npg * PAGE_SIZE // MOBA_BLOCK
    return pl.pallas_call(
        _block_mean_kernel,
        grid=(MOBA_H,),
        in_specs=[pl.BlockSpec((npg, 1, PAGE_SIZE, MOBA_HD), lambda h: (0, h, 0, 0))],
        out_specs=pl.BlockSpec((1, nb, MOBA_HD), lambda h: (h, 0, 0)),
        out_shape=jax.ShapeDtypeStruct((MOBA_H, nb, MOBA_HD), F32),
        compiler_params=_cparams("parallel"),
        name="moba_block_means",
    )(k_pages)


MOBA_GROUP = 4
MASK_BIAS = -1e30
LOG2E = 1.4426950408889634


def _select_bias_t(gate_t, n_valid):
    neg = jnp.float32(-jnp.inf)
    blk = lax.broadcasted_iota(jnp.int32, gate_t.shape, 0).astype(F32)
    g = jnp.where(blk < n_valid.astype(F32), gate_t, neg)
    bias = jnp.full(gate_t.shape, MASK_BIAS, F32)
    for _ in range(MOBA_TOPK):
        mx = jnp.max(g, axis=0, keepdims=True)
        idx = jnp.min(jnp.where(g == mx, blk, jnp.float32(1e9)), axis=0, keepdims=True)
        pick = (blk == idx) & (mx > neg)
        bias = jnp.where(pick, 0.0, bias)
        g = jnp.where(pick, neg, g)
    return bias


def _moba_prompt_kernel(q_ref, km_ref, kb_ref, vb_ref, o_ref, qa_ref, acc_ref, s_ref):
    qi = pl.program_id(1)
    ng, tq, _ = q_ref.shape
    scale = MOBA_HD ** -0.5
    neg = jnp.float32(-jnp.inf)
    rr = lax.broadcasted_iota(jnp.int32, (tq, MOBA_BLOCK), 0)
    cc = lax.broadcasted_iota(jnp.int32, (tq, MOBA_BLOCK), 1)
    causal = cc <= rr
    ones_v = jnp.ones((MOBA_BLOCK, MOBA_HD), BF16)
    start = pl.multiple_of(qi * MOBA_BLOCK, MOBA_BLOCK)

    heads = range(ng)
    lane_blk = lax.broadcasted_iota(jnp.int32, (MOBA_BLOCK, LANES), 1)

    def scores(g, j):
        st = pl.multiple_of(j * MOBA_BLOCK, MOBA_BLOCK)
        onehot = (lane_blk == j).astype(BF16)
        kj = jnp.concatenate([kb_ref[g, pl.ds(st, MOBA_BLOCK), :], onehot], axis=1)
        return lax.dot_general(qa_ref[g], kj, NT, preferred_element_type=F32)

    m0 = []
    for g in heads:
        q = q_ref[g]
        bias = _select_bias_t(_mm3_nt(km_ref[g], q), qi).T
        qs = (q * (scale * LOG2E)).astype(BF16)
        qa_ref[g] = jnp.concatenate([qs, bias.astype(BF16)], axis=1)
        kd = kb_ref[g, pl.ds(start, MOBA_BLOCK), :]
        vd = vb_ref[g, pl.ds(start, MOBA_BLOCK), :]
        s = jnp.where(causal, lax.dot_general(qs, kd, NT, preferred_element_type=F32), neg)
        m = jnp.max(s, axis=1, keepdims=True)
        p = jnp.exp2(s - m).astype(BF16)
        m0.append(m)
        acc_ref[g] = jnp.dot(p, jnp.concatenate([vd, ones_v], axis=1), preferred_element_type=F32)
    for g in heads:
        s_ref[0, g] = scores(g, 0)

    def body(j, ms):
        slot = j & 1
        s_next = [scores(g, j + 1) for g in heads]
        st = pl.multiple_of(j * MOBA_BLOCK, MOBA_BLOCK)
        s = [s_ref[slot, g] for g in heads]
        m_new = [jnp.maximum(ms[g], jnp.max(s[g], axis=1, keepdims=True)) for g in heads]
        p = [jnp.exp2(s[g] - m_new[g]).astype(BF16) for g in heads]
        pv = [jnp.dot(p[g], jnp.concatenate([vb_ref[g, pl.ds(st, MOBA_BLOCK), :], ones_v], axis=1),
                      preferred_element_type=F32) for g in heads]
        for g in heads:
            acc_ref[g] = jnp.exp2(ms[g] - m_new[g]) * acc_ref[g] + pv[g]
            s_ref[1 - slot, g] = s_next[g]
        return tuple(m_new)

    lax.fori_loop(0, qi, body, tuple(m0))
    for g in heads:
        acc = acc_ref[g]
        o_ref[:, g * MOBA_HD:(g + 1) * MOBA_HD] = (acc[:, :MOBA_HD] / acc[:, MOBA_HD:]).astype(o_ref.dtype)


def _moba_prompt(q3, km, kb, vb):
    t = q3.shape[1]
    nb = t // MOBA_BLOCK
    assert t % MOBA_BLOCK == 0 and nb <= LANES and MOBA_H % MOBA_GROUP == 0
    km_pad = jnp.concatenate([km, jnp.zeros((MOBA_H, LANES - nb, MOBA_HD), F32)], axis=1)
    ng = MOBA_GROUP
    resident = lambda: pl.BlockSpec((ng, t, MOBA_HD), lambda h, i: (h, 0, 0), pipeline_mode=pl.Buffered(1))
    return pl.pallas_call(
        _moba_prompt_kernel,
        grid=(MOBA_H // ng, nb),
        in_specs=[pl.BlockSpec((ng, MOBA_BLOCK, MOBA_HD), lambda h, i: (h, i, 0)),
                  pl.BlockSpec((ng, LANES, MOBA_HD), lambda h, i: (h, 0, 0)),
                  resident(), resident()],
        out_specs=pl.BlockSpec((MOBA_BLOCK, ng * MOBA_HD), lambda h, i: (i, h)),
        out_shape=jax.ShapeDtypeStruct((t, MOBA_W), BF16),
        scratch_shapes=[pltpu.VMEM((ng, MOBA_BLOCK, 2 * MOBA_HD), BF16),
                        pltpu.VMEM((ng, MOBA_BLOCK, 2 * MOBA_HD), F32),
                        pltpu.VMEM((2, ng, MOBA_BLOCK, MOBA_BLOCK), F32)],
        compiler_params=_cparams("parallel", "arbitrary"),
        name="moba_prompt",
    )(q3, km_pad, kb, vb)


PAGES_PER_STEP = 8


def _page_sum_kernel(pt_ref, *refs):
    del pt_ref
    k_refs, out_ref = refs[:PAGES_PER_STEP], refs[PAGES_PER_STEP]
    for h in range(MOBA_H):
        rows = [jnp.sum(k_refs[u][0, 0, h], axis=0, keepdims=True) for u in range(PAGES_PER_STEP)]
        out_ref[0, h] = jnp.concatenate(rows, axis=0)


def _page_sums(cache_k, page_table):
    nbt, n_pages = page_table.shape
    assert n_pages % PAGES_PER_STEP == 0
    spec = lambda u: pl.BlockSpec((1, 1, MOBA_H, PAGE_SIZE, MOBA_HD),
                                  lambda b, g, pt: (pt[b, g * PAGES_PER_STEP + u], 0, 0, 0, 0))
    return pl.pallas_call(
        _page_sum_kernel,
        grid_spec=pltpu.PrefetchScalarGridSpec(
            num_scalar_prefetch=1,
            grid=(nbt, n_pages // PAGES_PER_STEP),
            in_specs=[spec(u) for u in range(PAGES_PER_STEP)],
            out_specs=pl.BlockSpec((1, MOBA_H, PAGES_PER_STEP, MOBA_HD), lambda b, g, pt: (b, 0, g, 0))),
        out_shape=jax.ShapeDtypeStruct((nbt, MOBA_H, n_pages, MOBA_HD), F32),
        compiler_params=_cparams("parallel", "arbitrary"),
        name="moba_page_sums",
    )(page_table, *([cache_k] * PAGES_PER_STEP))


def _sample_gate_kernel(q_ref, ps_ref, sel_ref):
    n_pages = ps_ref.shape[2]
    lane = lax.broadcasted_iota(jnp.int32, (1, n_pages), 1)
    even = (lane & 1) == 0
    neg = jnp.float32(-jnp.inf)
    big = jnp.int32(2 ** 30)
    out_lane = lax.broadcasted_iota(jnp.int32, (1, LANES), 1)
    rows = []
    for h in range(MOBA_H):
        qh = jnp.broadcast_to(q_ref[0, h:h + 1, :], (8, MOBA_HD))
        gp = _mm3_nt(qh, ps_ref[0, h])[0:1, :]
        nbr = jnp.where(even, pltpu.roll(gp, n_pages - 1, axis=1), pltpu.roll(gp, 1, axis=1))
        g = jnp.where(even, (gp + nbr) * (1.0 / MOBA_BLOCK), neg)
        res = jnp.zeros((1, LANES), jnp.int32)
        for s in range(MOBA_TOPK):
            mx = jnp.max(g, axis=1, keepdims=True)
            idx = jnp.min(jnp.where(g == mx, lane, big), axis=1, keepdims=True)
            g = jnp.where(lane == idx, neg, g)
            res = jnp.where(out_lane == s, idx >> 1, res)
        rows.append(res)
    sel_ref[0] = jnp.concatenate(rows, axis=0)


def _sample_gate(q3, psums):
    nbt, _, n_pages, _ = psums.shape
    assert n_pages == LANES and n_pages // 2 >= MOBA_TOPK
    return pl.pallas_call(
        _sample_gate_kernel,
        grid=(nbt,),
        in_specs=[pl.BlockSpec((1, MOBA_H, MOBA_HD), lambda b: (b, 0, 0)),
                  pl.BlockSpec((1, MOBA_H, n_pages, MOBA_HD), lambda b: (b, 0, 0, 0))],
        out_specs=pl.BlockSpec((1, MOBA_H, LANES), lambda b: (b, 0, 0)),
        out_shape=jax.ShapeDtypeStruct((nbt, MOBA_H, LANES), jnp.int32),
        compiler_params=_cparams("parallel"),
        name="moba_sample_gate",
    )(q3, psums)


SAMPLE_PAGES = MOBA_TOPK * (MOBA_BLOCK // PAGE_SIZE)


def _sample_attn_kernel(ph_ref, q_ref, kn_ref, vn_ref, *refs):
    del ph_ref
    k_refs = refs[:SAMPLE_PAGES]
    v_refs = refs[SAMPLE_PAGES:2 * SAMPLE_PAGES]
    o_ref = refs[2 * SAMPLE_PAGES]
    scale = MOBA_HD ** -0.5
    q = q_ref[0, 0]
    s_self = jnp.sum(kn_ref[0, 0] * q, axis=1, keepdims=True) * scale
    ss = [jnp.sum(k_refs[u][0, 0, 0] * q, axis=1, keepdims=True) * scale for u in range(SAMPLE_PAGES)]
    m = s_self
    for s in ss:
        m = jnp.maximum(m, jnp.max(s, axis=0, keepdims=True))
    p_self = jnp.exp(s_self - m)
    l = p_self
    acc = p_self * vn_ref[0, 0]
    for u in range(SAMPLE_PAGES):
        p = jnp.exp(ss[u] - m)
        l = l + jnp.sum(p, axis=0, keepdims=True)
        acc = acc + jnp.sum(p * v_refs[u][0, 0, 0], axis=0, keepdims=True)
    o_ref[0, 0] = acc / l


def _sample_attn(q4, kn4, vn4, cache_k, cache_v, phys):
    nbt = q4.shape[0]
    vec_spec = pl.BlockSpec((1, 1, 1, MOBA_HD), lambda b, h, ph: (b, h, 0, 0))
    page_spec = lambda u: pl.BlockSpec(
        (1, 1, 1, PAGE_SIZE, MOBA_HD),
        lambda b, h, ph: (ph[(b * MOBA_H + h) * SAMPLE_PAGES + u], 0, h, 0, 0))
    pages = [page_spec(u) for u in range(SAMPLE_PAGES)]
    return pl.pallas_call(
        _sample_attn_kernel,
        grid_spec=pltpu.PrefetchScalarGridSpec(
            num_scalar_prefetch=1,
            grid=(nbt, MOBA_H),
            in_specs=[vec_spec, vec_spec, vec_spec] + pages + pages,
            out_specs=vec_spec),
        out_shape=jax.ShapeDtypeStruct((nbt, MOBA_H, 1, MOBA_HD), F32),
        compiler_params=_cparams("parallel", "arbitrary"),
        name="moba_sample_attn",
    )(phys, q4, kn4, vn4, *([cache_k] * SAMPLE_PAGES), *([cache_v] * SAMPLE_PAGES))


def _outproj_kernel(x_ref, or_ref, om_ref, wr_ref, wm_ref, g_ref, o_ref):
    y = (jnp.dot(or_ref[...], wr_ref[...], preferred_element_type=F32)
         + jnp.dot(om_ref[...], wm_ref[...], preferred_element_type=F32))
    o_ref[...] = x_ref[...] + _rms(y, g_ref[...])


def _outproj(x2, o_r, o_m, w_r, w_m, g):
    m = x2.shape[0]
    tm = min(m, 512)
    assert m % tm == 0
    return pl.pallas_call(
        _outproj_kernel,
        grid=(m // tm,),
        in_specs=[pl.BlockSpec((tm, D_MODEL), lambda i: (i, 0)),
                  pl.BlockSpec((tm, RWKV_W), lambda i: (i, 0)),
                  pl.BlockSpec((tm, MOBA_W), lambda i: (i, 0)),
                  _full((RWKV_W, D_MODEL)), _full((MOBA_W, D_MODEL)), _full((1, D_MODEL))],
        out_specs=pl.BlockSpec((tm, D_MODEL), lambda i: (i, 0)),
        out_shape=jax.ShapeDtypeStruct((m, D_MODEL), F32),
        compiler_params=_cparams("parallel"),
        name="outproj",
    )(x2, o_r, o_m, w_r, w_m, g)


MLP_TF = 1024


def _mlp_kernel(x_ref, g3_ref, g4_ref, wu_ref, wd_ref, o_ref, xn_ref, acc_ref):
    j = pl.program_id(1)

    @pl.when(j == 0)
    def _():
        xn_ref[...] = _rms(x_ref[...], g3_ref[...]).astype(BF16)
        acc_ref[...] = jnp.zeros_like(acc_ref)

    h = jnp.dot(xn_ref[...], wu_ref[...], preferred_element_type=F32)
    h = jnp.square(jnp.maximum(h, 0.0))
    acc_ref[...] += jnp.dot(h.astype(BF16), wd_ref[...], preferred_element_type=F32)

    @pl.when(j == pl.num_programs(1) - 1)
    def _():
        o_ref[...] = x_ref[...] + _rms(acc_ref[...], g4_ref[...])


def _mlp(x1, g3, g4, w_up, w_down):
    m = x1.shape[0]
    tm = min(m, 512)
    assert m % tm == 0
    return pl.pallas_call(
        _mlp_kernel,
        grid=(m // tm, D_FF // MLP_TF),
        in_specs=[pl.BlockSpec((tm, D_MODEL), lambda i, j: (i, 0)),
                  pl.BlockSpec((1, D_MODEL), lambda i, j: (0, 0)),
                  pl.BlockSpec((1, D_MODEL), lambda i, j: (0, 0)),
                  pl.BlockSpec((D_MODEL, MLP_TF), lambda i, j: (0, j)),
                  pl.BlockSpec((MLP_TF, D_MODEL), lambda i, j: (j, 0))],
        out_specs=pl.BlockSpec((tm, D_MODEL), lambda i, j: (i, 0)),
        out_shape=jax.ShapeDtypeStruct((m, D_MODEL), F32),
        scratch_shapes=[pltpu.VMEM((tm, D_MODEL), BF16), pltpu.VMEM((tm, D_MODEL), F32)],
        compiler_params=_cparams("parallel", "arbitrary"),
        name="mlp",
    )(x1, g3, g4, w_up, w_down)


def _pad_cols(a):
    z = lambda n: jnp.zeros(a.shape[:-1] + (n,), a.dtype)
    wd = a[..., 3 * RWKV_W:3 * RWKV_W + DECAY_LORA]
    ad = a[..., 3 * RWKV_W + DECAY_LORA:3 * RWKV_W + DECAY_LORA + AAA_LORA]
    gd = a[..., 3 * RWKV_W + DECAY_LORA + AAA_LORA:SHIFT_W]
    parts = [a[..., :3 * RWKV_W], wd, z(LANES - DECAY_LORA), ad, z(LANES - AAA_LORA),
             gd, z(2 * LANES - GATE_LORA)]
    if a.shape[-1] > SHIFT_W:
        parts.append(a[..., SHIFT_W:])
    return jnp.concatenate(parts, axis=-1)


def _unpad_cols(zr):
    return jnp.concatenate([zr[..., :3 * RWKV_W + DECAY_LORA], zr[..., AD_OFF:AD_OFF + AAA_LORA],
                            zr[..., GD_OFF:GD_OFF + GATE_LORA]], axis=-1)


def _pad_rows(a, n):
    return jnp.concatenate([a, jnp.zeros((n - a.shape[0],) + a.shape[1:], a.dtype)], axis=0)


def _unpair_state(s_bd):
    h0 = s_bd[:, :RWKV_HD, :RWKV_HD]
    h1 = s_bd[:, RWKV_HD:, RWKV_HD:]
    return jnp.stack([h0, h1], axis=1).reshape(RWKV_H, RWKV_HD, RWKV_HD)


def kernel(x_prompt, x_sample, cache_k, cache_v, page_table, state_wkv, state_shift, g_mix_pre, g_mix_post, g_ffn_pre, g_ffn_post, w_in, mu_shift, w0, w_lora2, a0, a_lora2, g_lora2, k_k, k_a, r_k, lnx_w, lnx_b, w_out, w_up, w_down):
    depth = w_in.shape[0]
    bsz, seq, _ = x_prompt.shape
    dbt, dseq, _ = x_sample.shape
    assert depth == 1 and bsz == 1 and dseq == 1, "kernels are written for the stated shapes"
    n_pages = page_table.shape[1]
    l = 0

    row = lambda a: a.reshape(1, -1)
    w_pad = _pad_cols(w_in[l]).astype(BF16)
    lane = jnp.arange(LANES)
    e_seg = ((lane[:, None] // RWKV_HD) == (lane[None, :] // RWKV_HD)).astype(BF16)
    ci = jnp.arange(CHUNK)
    tri = (ci[:, None] >= ci[None, :]).astype(BF16)
    rp = dict(mu=row(_pad_cols(mu_shift[l])), w0=row(w0[l]), a0=row(a0[l]), k_k=row(k_k[l]),
              k_a=row(k_a[l]), r_k=row(r_k[l]), lnx_w=row(lnx_w[l]), lnx_b=row(lnx_b[l]),
              wl=_pad_rows(w_lora2[l], LANES).astype(BF16), al=_pad_rows(a_lora2[l], LANES).astype(BF16),
              gl=_pad_rows(g_lora2[l], 2 * LANES).astype(BF16), e=e_seg, tri=tri)
    w_or = w_out[l, :RWKV_W].astype(BF16)
    w_om = w_out[l, RWKV_W:].astype(BF16)
    w_up_b = w_up[l].astype(BF16)
    w_down_b = w_down[l].astype(BF16)
    g1, g2, g3, g4 = row(g_mix_pre[l]), row(g_mix_post[l]), row(g_ffn_pre[l]), row(g_ffn_post[l])

    xp = x_prompt.reshape(seq, D_MODEL)
    zr, q3, k_pages, v_pages, kb, vb = _proj(xp, g1, w_pad, paged=True)
    o_r, s_bd = _rwkv_prompt(zr, rp)
    km = _block_means(k_pages)
    o_m = _moba_prompt(q3, km, kb, vb)
    x1 = _outproj(xp, o_r, o_m, w_or, w_om, g2)
    y_prompt = _mlp(x1, g3, g4, w_up_b, w_down_b).reshape(bsz, seq, D_MODEL)
    npg = seq // PAGE_SIZE
    k_prompt = k_pages.reshape(bsz, npg, 1, MOBA_H, PAGE_SIZE, MOBA_HD)
    v_prompt = v_pages.reshape(bsz, npg, 1, MOBA_H, PAGE_SIZE, MOBA_HD)
    wkv_prompt = _unpair_state(s_bd).reshape(1, bsz, RWKV_H, RWKV_HD, RWKV_HD)
    shift_prompt = _unpad_cols(zr[seq - 1:seq]).reshape(1, bsz, SHIFT_W)

    xs = x_sample.reshape(dbt, D_MODEL)
    zr_s, q_s, k_s, v_s = _proj(xs, g1, w_pad, paged=False)
    o_rs, wkv_s = _rwkv_sample(zr_s, _pad_cols(state_shift[l]), state_wkv[l], rp)
    psums = _page_sums(cache_k, page_table)
    sel = _sample_gate(q_s.reshape(dbt, MOBA_H, MOBA_HD), psums)[:, :, :MOBA_TOPK]
    ppb = MOBA_BLOCK // PAGE_SIZE
    logical = sel[..., None] * ppb + jnp.arange(ppb, dtype=jnp.int32)
    phys = jnp.take_along_axis(page_table, logical.reshape(dbt, -1), axis=1).reshape(-1)
    as4 = lambda u: u.reshape(dbt, MOBA_H, 1, MOBA_HD)
    o_ms = _sample_attn(as4(q_s), as4(k_s), as4(v_s), cache_k, cache_v, phys)
    o_ms = o_ms.reshape(dbt, MOBA_W).astype(BF16)
    x1s = _outproj(xs, o_rs, o_ms, w_or, w_om, g2)
    y_sample = _mlp(x1s, g3, g4, w_up_b, w_down_b).reshape(dbt, dseq, D_MODEL)
    k_sample = k_s.reshape(dbt, 1, MOBA_H, dseq, MOBA_HD)
    v_sample = v_s.reshape(dbt, 1, MOBA_H, dseq, MOBA_HD)
    wkv_sample = wkv_s.reshape(1, dbt, RWKV_H, RWKV_HD, RWKV_HD)
    shift_sample = _unpad_cols(zr_s).reshape(1, dbt, SHIFT_W)

    return (y_prompt, y_sample, k_prompt, v_prompt, wkv_prompt, shift_prompt,
            k_sample, v_sample, wkv_sample, shift_sample)
```

```python
import functools

import jax
import jax.numpy as jnp
from jax import lax
from jax.experimental import pallas as pl
from jax.experimental.pallas import tpu as pltpu

F32 = jnp.float32
BF16 = jnp.bfloat16

D_MODEL = 2048
RWKV_W = 1024
RWKV_HD = 64
RWKV_H = RWKV_W // RWKV_HD
DECAY_LORA = 64
AAA_LORA = 64
GATE_LORA = 160
MOBA_W = 1024
MOBA_HD = 128
MOBA_H = MOBA_W // MOBA_HD
MOBA_BLOCK = 256
MOBA_TOPK = 3
PAGE_SIZE = 128
D_FF = 4 * D_MODEL
RMS_EPS = 1e-6
GN_EPS = 64e-5
SHIFT_W = 3 * RWKV_W + DECAY_LORA + AAA_LORA + GATE_LORA

LANES = 128
VMEM_LIMIT_BYTES = 48 * 1024 * 1024

WD_OFF = 3 * RWKV_W
AD_OFF = WD_OFF + LANES
GD_OFF = AD_OFF + LANES
ZR_W = GD_OFF + 2 * LANES
PROJ_TN = 512
ZR_TILES = ZR_W // PROJ_TN
MOBA_TILES = MOBA_W // PROJ_TN
PROJ_PAD_W = ZR_W + 3 * MOBA_W
PROJ_TILES = PROJ_PAD_W // PROJ_TN

CHUNK = 64
RWKV_TB = 256
N_PAIRS = RWKV_W // LANES

NT = (((1,), (1,)), ((), ()))


def _cparams(*sem, vmem=VMEM_LIMIT_BYTES):
    return pltpu.CompilerParams(dimension_semantics=sem, vmem_limit_bytes=vmem)


def _mm(a, b):
    return jnp.dot(a.astype(BF16), b.astype(BF16), preferred_element_type=F32)


def _mm_nt(a, b):
    return lax.dot_general(a.astype(BF16), b.astype(BF16), NT, preferred_element_type=F32)


def _split2(x):
    hi = x.astype(BF16)
    lo = (x - hi.astype(F32)).astype(BF16)
    return hi, lo


def _mm3_nt(a, b):
    ah, al = _split2(a)
    bh, bl = _split2(b)
    d = lambda u, v: lax.dot_general(u, v, NT, preferred_element_type=F32)
    return d(ah, bh) + (d(ah, bl) + d(al, bh))


def _mm_exact_rhs(x, e):
    h0 = x.astype(BF16)
    r0 = x - h0.astype(F32)
    h1 = r0.astype(BF16)
    h2 = (r0 - h1.astype(F32)).astype(BF16)
    d = lambda u: jnp.dot(u, e, preferred_element_type=F32)
    return d(h0) + (d(h1) + d(h2))


def _mm_exact_lhs(e, x):
    h0 = x.astype(BF16)
    r0 = x - h0.astype(F32)
    h1 = r0.astype(BF16)
    h2 = (r0 - h1.astype(F32)).astype(BF16)
    d = lambda u: jnp.dot(e, u, preferred_element_type=F32)
    return d(h0) + (d(h1) + d(h2))


def _rms(x, g):
    ms = jnp.mean(x * x, axis=-1, keepdims=True)
    return (x * lax.rsqrt(ms + RMS_EPS)) * g


def _proj_kernel(x_ref, g_ref, w_ref, *refs, paged, tm):
    if paged:
        zr_ref, q_ref, kp_ref, vp_ref, kb_ref, vt_ref, xn_ref = refs
    else:
        zr_ref, q_ref, k_ref, v_ref, xn_ref = refs
    j = pl.program_id(1)

    @pl.when(j == 0)
    def _():
        xn_ref[...] = _rms(x_ref[...], g_ref[...]).astype(BF16)

    acc = jnp.dot(xn_ref[...], w_ref[...], preferred_element_type=F32)
    heads_per_tile = PROJ_TN // MOBA_HD
    q0, k0, v0 = ZR_TILES, ZR_TILES + MOBA_TILES, ZR_TILES + 2 * MOBA_TILES

    @pl.when(j < q0)
    def _():
        zr_ref[...] = acc

    if not paged:
        @pl.when((j >= q0) & (j < k0))
        def _():
            q_ref[...] = acc

        @pl.when((j >= k0) & (j < v0))
        def _():
            k_ref[...] = acc

        @pl.when(j >= v0)
        def _():
            v_ref[...] = acc
        return

    @pl.when((j >= q0) & (j < k0))
    def _():
        for hh in range(heads_per_tile):
            q_ref[hh] = acc[:, hh * MOBA_HD:(hh + 1) * MOBA_HD]

    def store_pages(page_ref, blk, hh):
        for pg in range(tm // PAGE_SIZE):
            page_ref[pg, hh] = blk[pg * PAGE_SIZE:(pg + 1) * PAGE_SIZE]

    @pl.when((j >= k0) & (j < v0))
    def _():
        for hh in range(heads_per_tile):
            blk = acc[:, hh * MOBA_HD:(hh + 1) * MOBA_HD]
            kb_ref[hh] = blk.astype(BF16)
            store_pages(kp_ref, blk, hh)

    @pl.when(j >= v0)
    def _():
        for hh in range(heads_per_tile):
            blk = acc[:, hh * MOBA_HD:(hh + 1) * MOBA_HD]
            store_pages(vp_ref, blk, hh)
            for b in range(tm // MOBA_BLOCK):
                vt_ref[hh, b] = blk[b * MOBA_BLOCK:(b + 1) * MOBA_BLOCK].T.astype(BF16)


def _proj(x2, g, w_pad, *, paged):
    m = x2.shape[0]
    tm = 1024 if paged else m
    assert m % tm == 0
    q0, k0, v0 = ZR_TILES, ZR_TILES + MOBA_TILES, ZR_TILES + 2 * MOBA_TILES
    hpt = PROJ_TN // MOBA_HD
    clip = lambda j, lo: jnp.clip(j - lo, 0, MOBA_TILES - 1)
    in_specs = [
        pl.BlockSpec((tm, D_MODEL), lambda i, j: (i, 0)),
        pl.BlockSpec((1, D_MODEL), lambda i, j: (0, 0)),
        pl.BlockSpec((D_MODEL, PROJ_TN), lambda i, j: (0, j)),
    ]
    zr_spec = pl.BlockSpec((tm, PROJ_TN), lambda i, j: (i, jnp.minimum(j, ZR_TILES - 1)))
    zr_shape = jax.ShapeDtypeStruct((m, ZR_W), F32)
    if paged:
        npg = m // PAGE_SIZE
        head_spec = lambda lo: pl.BlockSpec((hpt, tm, MOBA_HD), lambda i, j: (clip(j, lo), i, 0))
        page_spec = lambda lo: pl.BlockSpec((tm // PAGE_SIZE, hpt, PAGE_SIZE, MOBA_HD),
                                            lambda i, j: (i, clip(j, lo), 0, 0))
        vt_spec = pl.BlockSpec((hpt, tm // MOBA_BLOCK, MOBA_HD, MOBA_BLOCK),
                               lambda i, j: (clip(j, v0), i, 0, 0))
        out_specs = [zr_spec, head_spec(q0), page_spec(k0), page_spec(v0), head_spec(k0), vt_spec]
        out_shape = [zr_shape,
                     jax.ShapeDtypeStruct((MOBA_H, m, MOBA_HD), F32),
                     jax.ShapeDtypeStruct((npg, MOBA_H, PAGE_SIZE, MOBA_HD), F32),
                     jax.ShapeDtypeStruct((npg, MOBA_H, PAGE_SIZE, MOBA_HD), F32),
                     jax.ShapeDtypeStruct((MOBA_H, m, MOBA_HD), BF16),
                     jax.ShapeDtypeStruct((MOBA_H, m // MOBA_BLOCK, MOBA_HD, MOBA_BLOCK), BF16)]
    else:
        flat_spec = lambda lo: pl.BlockSpec((tm, PROJ_TN), lambda i, j: (i, clip(j, lo)))
        out_specs = [zr_spec, flat_spec(q0), flat_spec(k0), flat_spec(v0)]
        out_shape = [zr_shape] + [jax.ShapeDtypeStruct((m, MOBA_W), F32)] * 3
    return pl.pallas_call(
        functools.partial(_proj_kernel, paged=paged, tm=tm),
        grid=(m // tm, PROJ_TILES),
        in_specs=in_specs, out_specs=out_specs, out_shape=out_shape,
        scratch_shapes=[pltpu.VMEM((tm, D_MODEL), BF16)],
        compiler_params=_cparams("parallel", "arbitrary"),
        name="proj_paged" if paged else "proj_flat",
    )(x2, g, w_pad)


def _segsum(x, e):
    outs = [_mm_exact_rhs(x[:, g * LANES:(g + 1) * LANES], e) for g in range(x.shape[1] // LANES)]
    return jnp.concatenate(outs, axis=1)


def _softplus(x):
    return jnp.maximum(x, 0.0) + jnp.log1p(jnp.exp(-jnp.abs(x)))


def _sigmoid(x):
    return 1.0 / (1.0 + jnp.exp(-x))


def _rwkv_pre(z, zprev, prm, e):
    mu, w0, a0, k_k, k_a, wl, al, gl = prm
    zs = z + mu * (zprev - z)
    r = zs[:, 0:RWKV_W]
    k = zs[:, RWKV_W:2 * RWKV_W]
    v = zs[:, 2 * RWKV_W:3 * RWKV_W]
    wd = zs[:, WD_OFF:AD_OFF]
    ad = zs[:, AD_OFF:GD_OFF]
    gd = zs[:, GD_OFF:ZR_W]
    w_log = -_softplus(-(w0 + _mm(jnp.tanh(wd), wl))) - 0.5
    lw = -jnp.exp(w_log)
    a = _sigmoid(a0 + _mm(ad, al))
    g = _mm(_sigmoid(gd), gl)
    kk = k * k_k
    kk = kk * lax.rsqrt(jnp.maximum(_segsum(kk * kk, e), 1e-24))
    k2 = k * (1.0 + (a - 1.0) * k_a)
    b = kk * a
    return r, lw, k2, v, kk, b, g


def _rwkv_post(o, r, k2, v, g, r_k, lnx_w, lnx_b, e):
    inv_hd = 1.0 / RWKV_HD
    mean = _segsum(o, e) * inv_hd
    d = o - mean
    var = _segsum(d * d, e) * inv_hd
    on = d * lax.rsqrt(var + GN_EPS) * lnx_w + lnx_b
    on = on + _segsum(r * k2 * r_k, e) * v
    return on * g


def _rwkv_prompt_kernel(z_ref, mu_ref, w0_ref, a0_ref, kk_ref, ka_ref, rk_ref, lw_ref, lb_ref,
                        wl_ref, al_ref, gl_ref, e_ref, tri_ref,
                        o_ref, s_out_ref,
                        carry_ref, s_ref, r_s, lw_s, k_s, v_s, kk_s, b_s, o_s):
    i = pl.program_id(0)
    tb = z_ref.shape[0]

    @pl.when(i == 0)
    def _():
        carry_ref[...] = jnp.zeros_like(carry_ref)
        s_ref[...] = jnp.zeros_like(s_ref)

    e = e_ref[...]
    z = z_ref[...]
    row = lax.broadcasted_iota(jnp.int32, (tb, 1), 0)
    zprev = jnp.where(row == 0, carry_ref[...], pltpu.roll(z, 1, axis=0))
    carry_ref[...] = z[tb - 1:tb, :]
    prm = (mu_ref[...], w0_ref[...], a0_ref[...], kk_ref[...], ka_ref[...],
           wl_ref[...], al_ref[...], gl_ref[...])
    r, lw, k2, v, kk, b, g = _rwkv_pre(z, zprev, prm, e)
    for p in range(N_PAIRS):
        sl = slice(p * LANES, (p + 1) * LANES)
        r_s[p] = r[:, sl]
        lw_s[p] = lw[:, sl]
        k_s[p] = k2[:, sl]
        v_s[p] = v[:, sl]
        kk_s[p] = kk[:, sl]
        b_s[p] = b[:, sl]

    lane = lax.broadcasted_iota(jnp.int32, (1, LANES), 1)
    m0 = (lane < RWKV_HD).astype(F32)
    m1 = 1.0 - m0
    ri = lax.broadcasted_iota(jnp.int32, (LANES, LANES), 0)
    ci = lax.broadcasted_iota(jnp.int32, (LANES, LANES), 1)
    rim, cim = ri & (CHUNK - 1), ci & (CHUNK - 1)
    strict = rim > cim
    incl = rim >= cim
    eye = (ri == ci).astype(F32)
    tri = tri_ref[...]

    def stack(x):
        return jnp.concatenate([x * m0, x * m1], axis=0)

    pairs = range(N_PAIRS)

    def chunk_body(c, carry):
        rows = pl.ds(pl.multiple_of(c * CHUNK, CHUNK), CHUNK)
        cum = [_mm_exact_lhs(tri, lw_s[p, rows, :]) for p in pairs]
        ar, bk, bkh, v_st, dec = [], [], [], [], []
        for p in pairs:
            lw_ = lw_s[p, rows, :]
            k_ = k_s[p, rows, :]
            b_ = b_s[p, rows, :]
            tot = cum[p][CHUNK - 1:CHUNK, :]
            gi = jnp.exp(-cum[p])
            gt = jnp.exp(tot - cum[p])
            a_st = stack(-kk_s[p, rows, :] * jnp.exp(cum[p] - lw_))
            r_st = stack(r_s[p, rows, :] * jnp.exp(cum[p]))
            ar.append(jnp.concatenate([a_st, r_st], axis=0).astype(BF16))
            bk.append(jnp.concatenate([stack(b_ * gi), stack(k_ * gi)], axis=0).astype(BF16))
            bkh.append(jnp.concatenate([stack(b_ * gt), stack(k_ * gt)], axis=0).astype(BF16))
            v_st.append(stack(v_s[p, rows, :]))
            dec.append(jnp.exp(tot))
        gm = [_mm_nt(ar[p], bk[p]) for p in pairs]
        ah_rh = [_mm_nt(ar[p], s_ref[p]) for p in pairs]
        n = [jnp.where(strict, gm[p][:LANES, :LANES], 0.0) for p in pairs]
        akm = [jnp.where(strict, gm[p][:LANES, LANES:], 0.0) for p in pairs]
        rbk = [jnp.concatenate([jnp.where(incl, gm[p][LANES:, :LANES], 0.0),
                                jnp.where(incl, gm[p][LANES:, LANES:], 0.0)], axis=1).astype(BF16)
               for p in pairs]
        w = [_mm(akm[p], v_st[p]) for p in pairs]
        tinv = [eye + n[p] for p in pairs]
        pw = n
        for _ in range(5):
            pw = [_mm(pw[p], pw[p]) for p in pairs]
            tinv = [tinv[p] + _mm(tinv[p], pw[p]) for p in pairs]
        u_st = [_mm(tinv[p], ah_rh[p][:LANES] + w[p]) for p in pairs]
        uv = [jnp.concatenate([u_st[p], v_st[p]], axis=0).astype(BF16) for p in pairs]
        o_st = [ah_rh[p][LANES:] + _mm(rbk[p], uv[p]) for p in pairs]
        for p in pairs:
            o_s[p, rows, :] = o_st[p][:CHUNK] + o_st[p][CHUNK:]
        uv_t = [jnp.concatenate([u_st[p].T, v_st[p].T], axis=1) for p in pairs]
        s_new = [s_ref[p] * dec[p] + _mm(uv_t[p], bkh[p]) for p in pairs]
        for p in pairs:
            s_ref[p] = s_new[p]
        return carry

    lax.fori_loop(0, tb // CHUNK, chunk_body, 0)

    o = jnp.concatenate([o_s[p] for p in range(N_PAIRS)], axis=1)
    out = _rwkv_post(o, r, k2, v, g, rk_ref[...], lw_ref[...], lb_ref[...], e)
    o_ref[...] = out.astype(o_ref.dtype)

    @pl.when(i == pl.num_programs(0) - 1)
    def _():
        s_out_ref[...] = s_ref[...]


def _full(shape):
    nd = len(shape)
    return pl.BlockSpec(shape, lambda *_: (0,) * nd)


def _rwkv_prompt(zr, rp):
    t = zr.shape[0]
    tb = RWKV_TB
    assert t % tb == 0 and tb % CHUNK == 0
    vec = lambda n: _full((1, n))
    in_specs = [pl.BlockSpec((tb, ZR_W), lambda i: (i, 0)),
                vec(ZR_W), vec(RWKV_W), vec(RWKV_W), vec(RWKV_W), vec(RWKV_W), vec(RWKV_W),
                vec(RWKV_W), vec(RWKV_W),
                _full((LANES, RWKV_W)), _full((LANES, RWKV_W)), _full((2 * LANES, RWKV_W)),
                _full((LANES, LANES)), _full((CHUNK, CHUNK))]
    pair_buf = pltpu.VMEM((N_PAIRS, tb, LANES), F32)
    return pl.pallas_call(
        _rwkv_prompt_kernel,
        grid=(t // tb,),
        in_specs=in_specs,
        out_specs=[pl.BlockSpec((tb, RWKV_W), lambda i: (i, 0)), _full((N_PAIRS, LANES, LANES))],
        out_shape=[jax.ShapeDtypeStruct((t, RWKV_W), BF16),
                   jax.ShapeDtypeStruct((N_PAIRS, LANES, LANES), F32)],
        scratch_shapes=[pltpu.VMEM((1, ZR_W), F32), pltpu.VMEM((N_PAIRS, LANES, LANES), F32)]
                       + [pair_buf] * 7,
        compiler_params=_cparams("arbitrary"),
        name="rwkv_prompt",
    )(zr, rp["mu"], rp["w0"], rp["a0"], rp["k_k"], rp["k_a"], rp["r_k"], rp["lnx_w"], rp["lnx_b"],
      rp["wl"], rp["al"], rp["gl"], rp["e"], rp["tri"])


def _rwkv_sample_pre_kernel(z_ref, prev_ref, mu_ref, w0_ref, a0_ref, kk_ref, ka_ref,
                            wl_ref, al_ref, gl_ref, e_ref,
                            r_o, w_o, k_o, v_o, kk_o, b_o, g_o):
    prm = (mu_ref[...], w0_ref[...], a0_ref[...], kk_ref[...], ka_ref[...],
           wl_ref[...], al_ref[...], gl_ref[...])
    r, lw, k2, v, kk, b, g = _rwkv_pre(z_ref[...], prev_ref[...], prm, e_ref[...])
    r_o[...] = r
    w_o[...] = jnp.exp(lw)
    k_o[...] = k2
    v_o[...] = v
    kk_o[...] = kk
    b_o[...] = b
    g_o[...] = g


def _rwkv_step_kernel(s_ref, r_ref, w_ref, k_ref, kk_ref, b_ref, v_ref, s_out, o_out):
    s = s_ref[0]
    sa = jnp.sum(s * (-kk_ref[0]), axis=-1, keepdims=True)
    s_new = s * w_ref[0] + sa * b_ref[0] + v_ref[0] * k_ref[0]
    s_out[0] = s_new
    o_out[0] = jnp.sum(s_new * r_ref[0], axis=-1, keepdims=True)


def _rwkv_sample_post_kernel(o_ref, r_ref, k_ref, v_ref, g_ref, rk_ref, lw_ref, lb_ref, e_ref, out_ref):
    out = _rwkv_post(o_ref[...], r_ref[...], k_ref[...], v_ref[...], g_ref[...],
                     rk_ref[...], lw_ref[...], lb_ref[...], e_ref[...])
    out_ref[...] = out.astype(out_ref.dtype)


def _rwkv_sample(zr, prev_pad, s0, rp):
    n = zr.shape[0]
    vec = lambda w: _full((1, w))
    rows = jax.ShapeDtypeStruct((n, RWKV_W), F32)
    r, w, k2, v, kk, b, g = pl.pallas_call(
        _rwkv_sample_pre_kernel,
        grid=(1,),
        in_specs=[_full((n, ZR_W)), _full((n, ZR_W)), vec(ZR_W), vec(RWKV_W), vec(RWKV_W), vec(RWKV_W),
                  vec(RWKV_W), _full((LANES, RWKV_W)), _full((LANES, RWKV_W)), _full((2 * LANES, RWKV_W)),
                  _full((LANES, LANES))],
        out_specs=[_full((n, RWKV_W))] * 7,
        out_shape=[rows] * 7,
        compiler_params=_cparams("arbitrary"),
        name="rwkv_sample_pre",
    )(zr, prev_pad, rp["mu"], rp["w0"], rp["a0"], rp["k_k"], rp["k_a"], rp["wl"], rp["al"], rp["gl"], rp["e"])

    as_row = lambda u: u.reshape(n, RWKV_H, 1, RWKV_HD)
    as_col = lambda u: u.reshape(n, RWKV_H, RWKV_HD, 1)
    row_spec = pl.BlockSpec((1, RWKV_H, 1, RWKV_HD), lambda i: (i, 0, 0, 0))
    col_spec = pl.BlockSpec((1, RWKV_H, RWKV_HD, 1), lambda i: (i, 0, 0, 0))
    st_spec = pl.BlockSpec((1, RWKV_H, RWKV_HD, RWKV_HD), lambda i: (i, 0, 0, 0))
    s_new, o = pl.pallas_call(
        _rwkv_step_kernel,
        grid=(n,),
        in_specs=[st_spec] + [row_spec] * 5 + [col_spec],
        out_specs=[st_spec, col_spec],
        out_shape=[jax.ShapeDtypeStruct((n, RWKV_H, RWKV_HD, RWKV_HD), F32),
                   jax.ShapeDtypeStruct((n, RWKV_H, RWKV_HD, 1), F32)],
        compiler_params=_cparams("parallel"),
        name="rwkv_sample_step",
    )(s0, as_row(r), as_row(w), as_row(k2), as_row(kk), as_row(b), as_col(v))

    out = pl.pallas_call(
        _rwkv_sample_post_kernel,
        grid=(1,),
        in_specs=[_full((n, RWKV_W))] * 5 + [vec(RWKV_W)] * 3 + [_full((LANES, LANES))],
        out_specs=_full((n, RWKV_W)),
        out_shape=jax.ShapeDtypeStruct((n, RWKV_W), BF16),
        compiler_params=_cparams("arbitrary"),
        name="rwkv_sample_post",
    )(o.reshape(n, RWKV_W), r, k2, v, g, rp["r_k"], rp["lnx_w"], rp["lnx_b"], rp["e"])
    return out, s_new


def _block_mean_kernel(k_ref, km_ref):
    x = k_ref[...]
    npg = x.shape[0]
    ppb = MOBA_BLOCK // PAGE_SIZE
    x = x.reshape(npg // ppb, MOBA_BLOCK, MOBA_HD)
    km_ref[0] = jnp.sum(x, axis=1) * (1.0 / MOBA_BLOCK)


def _block_means(k_pages):
    npg = k_pages.shape[0]
    nb = npg * PAGE_SIZE // MOBA_BLOCK
    return pl.pallas_call(
        _block_mean_kernel,
        grid=(MOBA_H,),
        in_specs=[pl.BlockSpec((npg, 1, PAGE_SIZE, MOBA_HD), lambda h: (0, h, 0, 0))],
        out_specs=pl.BlockSpec((1, nb, MOBA_HD), lambda h: (h, 0, 0)),
        out_shape=jax.ShapeDtypeStruct((MOBA_H, nb, MOBA_HD), F32),
        compiler_params=_cparams("parallel"),
        name="moba_block_means",
    )(k_pages)


MOBA_GROUP = 4
MASK_BIAS = -1e30
LOG2E = 1.4426950408889634
SUM_ROWS = 16


def _select_bias_t(gate_t, n_valid):
    neg = jnp.float32(-jnp.inf)
    blk = lax.broadcasted_iota(jnp.int32, gate_t.shape, 0).astype(F32)
    g = jnp.where(blk < n_valid.astype(F32), gate_t, neg)
    bias = jnp.full(gate_t.shape, MASK_BIAS, F32)
    for _ in range(MOBA_TOPK):
        mx = jnp.max(g, axis=0, keepdims=True)
        idx = jnp.min(jnp.where(g == mx, blk, jnp.float32(1e9)), axis=0, keepdims=True)
        pick = (blk == idx) & (mx > neg)
        bias = jnp.where(pick, 0.0, bias)
        g = jnp.where(pick, neg, g)
    return bias


def _moba_prompt_kernel(q_ref, km_ref, kb_ref, vt_ref, o_ref, qa_ref, acc_ref, s_ref):
    qi = pl.program_id(1)
    ng, tq, _ = q_ref.shape
    scale = MOBA_HD ** -0.5
    neg = jnp.float32(-jnp.inf)
    key_i = lax.broadcasted_iota(jnp.int32, (MOBA_BLOCK, tq), 0)
    qry_i = lax.broadcasted_iota(jnp.int32, (MOBA_BLOCK, tq), 1)
    causal = key_i <= qry_i
    ones_v = jnp.ones((SUM_ROWS, MOBA_BLOCK), BF16)
    start = pl.multiple_of(qi * MOBA_BLOCK, MOBA_BLOCK)

    heads = range(ng)
    lane_blk = lax.broadcasted_iota(jnp.int32, (MOBA_BLOCK, LANES), 1)

    def scores(g, j):
        st = pl.multiple_of(j * MOBA_BLOCK, MOBA_BLOCK)
        onehot = (lane_blk == j).astype(BF16)
        kj = jnp.concatenate([kb_ref[g, pl.ds(st, MOBA_BLOCK), :], onehot], axis=1)
        return jnp.dot(kj, qa_ref[g], preferred_element_type=F32)

    def values(g, j):
        return jnp.concatenate([vt_ref[g, j], ones_v], axis=0)

    m0 = []
    for g in heads:
        q = q_ref[g]
        bias_t = _select_bias_t(_mm3_nt(km_ref[g], q), qi)
        qt = (q * (scale * LOG2E)).T.astype(BF16)
        qa_ref[g] = jnp.concatenate([qt, bias_t.astype(BF16)], axis=0)
        kd = kb_ref[g, pl.ds(start, MOBA_BLOCK), :]
        s = jnp.where(causal, jnp.dot(kd, qt, preferred_element_type=F32), neg)
        m = jnp.max(s, axis=0, keepdims=True)
        p = jnp.exp2(s - m).astype(BF16)
        m0.append(m)
        acc_ref[g] = jnp.dot(values(g, qi), p, preferred_element_type=F32)
    for g in heads:
        s_ref[0, g] = scores(g, 0)

    def body(j, ms):
        slot = j & 1
        s_next = [scores(g, j + 1) for g in heads]
        s = [s_ref[slot, g] for g in heads]
        m_new = [jnp.maximum(ms[g], jnp.max(s[g], axis=0, keepdims=True)) for g in heads]
        p = [jnp.exp2(s[g] - m_new[g]).astype(BF16) for g in heads]
        pv = [jnp.dot(values(g, j), p[g], preferred_element_type=F32) for g in heads]
        for g in heads:
            acc_ref[g] = jnp.exp2(ms[g] - m_new[g]) * acc_ref[g] + pv[g]
            s_ref[1 - slot, g] = s_next[g]
        return tuple(m_new)

    lax.fori_loop(0, qi, body, tuple(m0))
    for g in heads:
        acc = acc_ref[g]
        out_t = acc[:MOBA_HD] / acc[MOBA_HD:MOBA_HD + 1]
        o_ref[:, g * MOBA_HD:(g + 1) * MOBA_HD] = out_t.T.astype(o_ref.dtype)


def _moba_prompt(q3, km, kb, vt):
    t = q3.shape[1]
    nb = t // MOBA_BLOCK
    assert t % MOBA_BLOCK == 0 and nb <= LANES and MOBA_H % MOBA_GROUP == 0
    km_pad = jnp.concatenate([km, jnp.zeros((MOBA_H, LANES - nb, MOBA_HD), F32)], axis=1)
    ng = MOBA_GROUP
    return pl.pallas_call(
        _moba_prompt_kernel,
        grid=(MOBA_H // ng, nb),
        in_specs=[pl.BlockSpec((ng, MOBA_BLOCK, MOBA_HD), lambda h, i: (h, i, 0)),
                  pl.BlockSpec((ng, LANES, MOBA_HD), lambda h, i: (h, 0, 0)),
                  pl.BlockSpec((ng, t, MOBA_HD), lambda h, i: (h, 0, 0), pipeline_mode=pl.Buffered(1)),
                  pl.BlockSpec((ng, nb, MOBA_HD, MOBA_BLOCK), lambda h, i: (h, 0, 0, 0),
                               pipeline_mode=pl.Buffered(1))],
        out_specs=pl.BlockSpec((MOBA_BLOCK, ng * MOBA_HD), lambda h, i: (i, h)),
        out_shape=jax.ShapeDtypeStruct((t, MOBA_W), BF16),
        scratch_shapes=[pltpu.VMEM((ng, MOBA_HD + LANES, MOBA_BLOCK), BF16),
                        pltpu.VMEM((ng, MOBA_HD + SUM_ROWS, MOBA_BLOCK), F32),
                        pltpu.VMEM((2, ng, MOBA_BLOCK, MOBA_BLOCK), F32)],
        compiler_params=_cparams("parallel", "arbitrary"),
        name="moba_prompt",
    )(q3, km_pad, kb, vt)


PAGES_PER_STEP = 8


def _page_sum_kernel(pt_ref, *refs):
    del pt_ref
    k_refs, out_ref = refs[:PAGES_PER_STEP], refs[PAGES_PER_STEP]
    for h in range(MOBA_H):
        rows = [jnp.sum(k_refs[u][0, 0, h], axis=0, keepdims=True) for u in range(PAGES_PER_STEP)]
        out_ref[0, h] = jnp.concatenate(rows, axis=0)


def _page_sums(cache_k, page_table):
    nbt, n_pages = page_table.shape
    assert n_pages % PAGES_PER_STEP == 0
    spec = lambda u: pl.BlockSpec((1, 1, MOBA_H, PAGE_SIZE, MOBA_HD),
                                  lambda b, g, pt: (pt[b, g * PAGES_PER_STEP + u], 0, 0, 0, 0))
    return pl.pallas_call(
        _page_sum_kernel,
        grid_spec=pltpu.PrefetchScalarGridSpec(
            num_scalar_prefetch=1,
            grid=(nbt, n_pages // PAGES_PER_STEP),
            in_specs=[spec(u) for u in range(PAGES_PER_STEP)],
            out_specs=pl.BlockSpec((1, MOBA_H, PAGES_PER_STEP, MOBA_HD), lambda b, g, pt: (b, 0, g, 0))),
        out_shape=jax.ShapeDtypeStruct((nbt, MOBA_H, n_pages, MOBA_HD), F32),
        compiler_params=_cparams("parallel", "arbitrary"),
        name="moba_page_sums",
    )(page_table, *([cache_k] * PAGES_PER_STEP))


def _sample_gate_kernel(q_ref, ps_ref, sel_ref):
    n_pages = ps_ref.shape[2]
    lane = lax.broadcasted_iota(jnp.int32, (1, n_pages), 1)
    even = (lane & 1) == 0
    neg = jnp.float32(-jnp.inf)
    big = jnp.int32(2 ** 30)
    out_lane = lax.broadcasted_iota(jnp.int32, (1, LANES), 1)
    rows = []
    for h in range(MOBA_H):
        qh = jnp.broadcast_to(q_ref[0, h:h + 1, :], (8, MOBA_HD))
        gp = _mm3_nt(qh, ps_ref[0, h])[0:1, :]
        nbr = jnp.where(even, pltpu.roll(gp, n_pages - 1, axis=1), pltpu.roll(gp, 1, axis=1))
        g = jnp.where(even, (gp + nbr) * (1.0 / MOBA_BLOCK), neg)
        res = jnp.zeros((1, LANES), jnp.int32)
        for s in range(MOBA_TOPK):
            mx = jnp.max(g, axis=1, keepdims=True)
            idx = jnp.min(jnp.where(g == mx, lane, big), axis=1, keepdims=True)
            g = jnp.where(lane == idx, neg, g)
            res = jnp.where(out_lane == s, idx >> 1, res)
        rows.append(res)
    sel_ref[0] = jnp.concatenate(rows, axis=0)


def _sample_gate(q3, psums):
    nbt, _, n_pages, _ = psums.shape
    assert n_pages == LANES and n_pages // 2 >= MOBA_TOPK
    return pl.pallas_call(
        _sample_gate_kernel,
        grid=(nbt,),
        in_specs=[pl.BlockSpec((1, MOBA_H, MOBA_HD), lambda b: (b, 0, 0)),
                  pl.BlockSpec((1, MOBA_H, n_pages, MOBA_HD), lambda b: (b, 0, 0, 0))],
        out_specs=pl.BlockSpec((1, MOBA_H, LANES), lambda b: (b, 0, 0)),
        out_shape=jax.ShapeDtypeStruct((nbt, MOBA_H, LANES), jnp.int32),
        compiler_params=_cparams("parallel"),
        name="moba_sample_gate",
    )(q3, psums)


SAMPLE_PAGES = MOBA_TOPK * (MOBA_BLOCK // PAGE_SIZE)


def _sample_attn_kernel(ph_ref, q_ref, kn_ref, vn_ref, *refs):
    del ph_ref
    k_refs = refs[:SAMPLE_PAGES]
    v_refs = refs[SAMPLE_PAGES:2 * SAMPLE_PAGES]
    o_ref = refs[2 * SAMPLE_PAGES]
    scale = MOBA_HD ** -0.5
    q = q_ref[0, 0]
    s_self = jnp.sum(kn_ref[0, 0] * q, axis=1, keepdims=True) * scale
    ss = [jnp.sum(k_refs[u][0, 0, 0] * q, axis=1, keepdims=True) * scale for u in range(SAMPLE_PAGES)]
    m = s_self
    for s in ss:
        m = jnp.maximum(m, jnp.max(s, axis=0, keepdims=True))
    p_self = jnp.exp(s_self - m)
    l = p_self
    acc = p_self * vn_ref[0, 0]
    for u in range(SAMPLE_PAGES):
        p = jnp.exp(ss[u] - m)
        l = l + jnp.sum(p, axis=0, keepdims=True)
        acc = acc + jnp.sum(p * v_refs[u][0, 0, 0], axis=0, keepdims=True)
    o_ref[0, 0] = acc / l


def _sample_attn(q4, kn4, vn4, cache_k, cache_v, phys):
    nbt = q4.shape[0]
    vec_spec = pl.BlockSpec((1, 1, 1, MOBA_HD), lambda b, h, ph: (b, h, 0, 0))
    page_spec = lambda u: pl.BlockSpec(
        (1, 1, 1, PAGE_SIZE, MOBA_HD),
        lambda b, h, ph: (ph[(b * MOBA_H + h) * SAMPLE_PAGES + u], 0, h, 0, 0))
    pages = [page_spec(u) for u in range(SAMPLE_PAGES)]
    return pl.pallas_call(
        _sample_attn_kernel,
        grid_spec=pltpu.PrefetchScalarGridSpec(
            num_scalar_prefetch=1,
            grid=(nbt, MOBA_H),
            in_specs=[vec_spec, vec_spec, vec_spec] + pages + pages,
            out_specs=vec_spec),
        out_shape=jax.ShapeDtypeStruct((nbt, MOBA_H, 1, MOBA_HD), F32),
        compiler_params=_cparams("parallel", "arbitrary"),
        name="moba_sample_attn",
    )(phys, q4, kn4, vn4, *([cache_k] * SAMPLE_PAGES), *([cache_v] * SAMPLE_PAGES))


def _outproj_kernel(x_ref, or_ref, om_ref, wr_ref, wm_ref, g_ref, o_ref):
    y = (jnp.dot(or_ref[...], wr_ref[...], preferred_element_type=F32)
         + jnp.dot(om_ref[...], wm_ref[...], preferred_element_type=F32))
    o_ref[...] = x_ref[...] + _rms(y, g_ref[...])


def _outproj(x2, o_r, o_m, w_r, w_m, g):
    m = x2.shape[0]
    tm = min(m, 512)
    assert m % tm == 0
    return pl.pallas_call(
        _outproj_kernel,
        grid=(m // tm,),
        in_specs=[pl.BlockSpec((tm, D_MODEL), lambda i: (i, 0)),
                  pl.BlockSpec((tm, RWKV_W), lambda i: (i, 0)),
                  pl.BlockSpec((tm, MOBA_W), lambda i: (i, 0)),
                  _full((RWKV_W, D_MODEL)), _full((MOBA_W, D_MODEL)), _full((1, D_MODEL))],
        out_specs=pl.BlockSpec((tm, D_MODEL), lambda i: (i, 0)),
        out_shape=jax.ShapeDtypeStruct((m, D_MODEL), F32),
        compiler_params=_cparams("parallel"),
        name="outproj",
    )(x2, o_r, o_m, w_r, w_m, g)


MLP_TF = 512
MLP_TM = 1024
MLP_VMEM_LIMIT_BYTES = 56 * 1024 * 1024


def _mlp_kernel(x_ref, g3_ref, g4_ref, wu_ref, wd_ref, o_ref, xn_ref):
    j = pl.program_id(1)

    @pl.when(j == 0)
    def _():
        xn_ref[...] = _rms(x_ref[...], g3_ref[...]).astype(BF16)

        o_ref[...] = jnp.zeros_like(o_ref)

    h = jnp.dot(xn_ref[...], wu_ref[...], preferred_element_type=F32)
    h = jnp.square(jnp.maximum(h, 0.0))
    o_ref[...] += jnp.dot(h.astype(BF16), wd_ref[...], preferred_element_type=F32)

    @pl.when(j == pl.num_programs(1) - 1)
    def _():
        o_ref[...] = x_ref[...] + _rms(o_ref[...], g4_ref[...])


def _mlp(x1, g3, g4, w_up, w_down):
    m = x1.shape[0]
    tm = min(m, MLP_TM)
    assert m % tm == 0
    return pl.pallas_call(
        _mlp_kernel,
        grid=(m // tm, D_FF // MLP_TF),
        in_specs=[pl.BlockSpec((tm, D_MODEL), lambda i, j: (i, 0)),
                  pl.BlockSpec((1, D_MODEL), lambda i, j: (0, 0)),
                  pl.BlockSpec((1, D_MODEL), lambda i, j: (0, 0)),
                  pl.BlockSpec((D_MODEL, MLP_TF), lambda i, j: (0, j)),
                  pl.BlockSpec((MLP_TF, D_MODEL), lambda i, j: (j, 0))],
        out_specs=pl.BlockSpec((tm, D_MODEL), lambda i, j: (i, 0)),
        out_shape=jax.ShapeDtypeStruct((m, D_MODEL), F32),
        scratch_shapes=[pltpu.VMEM((tm, D_MODEL), BF16)],
        compiler_params=_cparams("parallel", "arbitrary", vmem=MLP_VMEM_LIMIT_BYTES),
        name="mlp",
    )(x1, g3, g4, w_up, w_down)


def _pad_cols(a):
    z = lambda n: jnp.zeros(a.shape[:-1] + (n,), a.dtype)
    wd = a[..., 3 * RWKV_W:3 * RWKV_W + DECAY_LORA]
    ad = a[..., 3 * RWKV_W + DECAY_LORA:3 * RWKV_W + DECAY_LORA + AAA_LORA]
    gd = a[..., 3 * RWKV_W + DECAY_LORA + AAA_LORA:SHIFT_W]
    parts = [a[..., :3 * RWKV_W], wd, z(LANES - DECAY_LORA), ad, z(LANES - AAA_LORA),
             gd, z(2 * LANES - GATE_LORA)]
    if a.shape[-1] > SHIFT_W:
        parts.append(a[..., SHIFT_W:])
    return jnp.concatenate(parts, axis=-1)


def _unpad_cols(zr):
    return jnp.concatenate([zr[..., :3 * RWKV_W + DECAY_LORA], zr[..., AD_OFF:AD_OFF + AAA_LORA],
                            zr[..., GD_OFF:GD_OFF + GATE_LORA]], axis=-1)


def _pad_rows(a, n):
    return jnp.concatenate([a, jnp.zeros((n - a.shape[0],) + a.shape[1:], a.dtype)], axis=0)


def _unpair_state(s_bd):
    h0 = s_bd[:, :RWKV_HD, :RWKV_HD]
    h1 = s_bd[:, RWKV_HD:, RWKV_HD:]
    return jnp.stack([h0, h1], axis=1).reshape(RWKV_H, RWKV_HD, RWKV_HD)


def kernel(x_prompt, x_sample, cache_k, cache_v, page_table, state_wkv, state_shift, g_mix_pre, g_mix_post, g_ffn_pre, g_ffn_post, w_in, mu_shift, w0, w_lora2, a0, a_lora2, g_lora2, k_k, k_a, r_k, lnx_w, lnx_b, w_out, w_up, w_down):
    depth = w_in.shape[0]
    bsz, seq, _ = x_prompt.shape
    dbt, dseq, _ = x_sample.shape
    assert depth == 1 and bsz == 1 and dseq == 1, "kernels are written for the stated shapes"
    n_pages = page_table.shape[1]
    l = 0

    row = lambda a: a.reshape(1, -1)
    w_pad = _pad_cols(w_in[l]).astype(BF16)
    lane = jnp.arange(LANES)
    e_seg = ((lane[:, None] // RWKV_HD) == (lane[None, :] // RWKV_HD)).astype(BF16)
    ci = jnp.arange(CHUNK)
    tri = (ci[:, None] >= ci[None, :]).astype(BF16)
    rp = dict(mu=row(_pad_cols(mu_shift[l])), w0=row(w0[l]), a0=row(a0[l]), k_k=row(k_k[l]),
              k_a=row(k_a[l]), r_k=row(r_k[l]), lnx_w=row(lnx_w[l]), lnx_b=row(lnx_b[l]),
              wl=_pad_rows(w_lora2[l], LANES).astype(BF16), al=_pad_rows(a_lora2[l], LANES).astype(BF16),
              gl=_pad_rows(g_lora2[l], 2 * LANES).astype(BF16), e=e_seg, tri=tri)
    w_or = w_out[l, :RWKV_W].astype(BF16)
    w_om = w_out[l, RWKV_W:].astype(BF16)
    w_up_b = w_up[l].astype(BF16)
    w_down_b = w_down[l].astype(BF16)
    g1, g2, g3, g4 = row(g_mix_pre[l]), row(g_mix_post[l]), row(g_ffn_pre[l]), row(g_ffn_post[l])

    xp = x_prompt.reshape(seq, D_MODEL)
    zr, q3, k_pages, v_pages, kb, vt = _proj(xp, g1, w_pad, paged=True)
    o_r, s_bd = _rwkv_prompt(zr, rp)
    km = _block_means(k_pages)
    o_m = _moba_prompt(q3, km, kb, vt)
    x1 = _outproj(xp, o_r, o_m, w_or, w_om, g2)
    y_prompt = _mlp(x1, g3, g4, w_up_b, w_down_b).reshape(bsz, seq, D_MODEL)
    npg = seq // PAGE_SIZE
    k_prompt = k_pages.reshape(bsz, npg, 1, MOBA_H, PAGE_SIZE, MOBA_HD)
    v_prompt = v_pages.reshape(bsz, npg, 1, MOBA_H, PAGE_SIZE, MOBA_HD)
    wkv_prompt = _unpair_state(s_bd).reshape(1, bsz, RWKV_H, RWKV_HD, RWKV_HD)
    shift_prompt = _unpad_cols(zr[seq - 1:seq]).reshape(1, bsz, SHIFT_W)

    xs = x_sample.reshape(dbt, D_MODEL)
    zr_s, q_s, k_s, v_s = _proj(xs, g1, w_pad, paged=False)
    o_rs, wkv_s = _rwkv_sample(zr_s, _pad_cols(state_shift[l]), state_wkv[l], rp)
    psums = _page_sums(cache_k, page_table)
    sel = _sample_gate(q_s.reshape(dbt, MOBA_H, MOBA_HD), psums)[:, :, :MOBA_TOPK]
    ppb = MOBA_BLOCK // PAGE_SIZE
    logical = sel[..., None] * ppb + jnp.arange(ppb, dtype=jnp.int32)
    phys = jnp.take_along_axis(page_table, logical.reshape(dbt, -1), axis=1).reshape(-1)
    as4 = lambda u: u.reshape(dbt, MOBA_H, 1, MOBA_HD)
    o_ms = _sample_attn(as4(q_s), as4(k_s), as4(v_s), cache_k, cache_v, phys)
    o_ms = o_ms.reshape(dbt, MOBA_W).astype(BF16)
    x1s = _outproj(xs, o_rs, o_ms, w_or, w_om, g2)
    y_sample = _mlp(x1s, g3, g4, w_up_b, w_down_b).reshape(dbt, dseq, D_MODEL)
    k_sample = k_s.reshape(dbt, 1, MOBA_H, dseq, MOBA_HD)
    v_sample = v_s.reshape(dbt, 1, MOBA_H, dseq, MOBA_HD)
    wkv_sample = wkv_s.reshape(1, dbt, RWKV_H, RWKV_HD, RWKV_HD)
    shift_sample = _unpad_cols(zr_s).reshape(1, dbt, SHIFT_W)

    return (y_prompt, y_sample, k_prompt, v_prompt, wkv_prompt, shift_prompt,
            k_sample, v_sample, wkv_sample, shift_sample)
```

```python
import functools

import jax
import jax.numpy as jnp
from jax import lax
from jax.experimental import pallas as pl
from jax.experimental.pallas import tpu as pltpu

F32 = jnp.float32
BF16 = jnp.bfloat16

D_MODEL = 2048
RWKV_W = 1024
RWKV_HD = 64
RWKV_H = RWKV_W // RWKV_HD
DECAY_LORA = 64
AAA_LORA = 64
GATE_LORA = 160
MOBA_W = 1024
MOBA_HD = 128
MOBA_H = MOBA_W // MOBA_HD
MOBA_BLOCK = 256
MOBA_TOPK = 3
PAGE_SIZE = 128
D_FF = 4 * D_MODEL
RMS_EPS = 1e-6
GN_EPS = 64e-5
SHIFT_W = 3 * RWKV_W + DECAY_LORA + AAA_LORA + GATE_LORA

LANES = 128
VMEM_LIMIT_BYTES = 48 * 1024 * 1024

WD_OFF = 3 * RWKV_W
AD_OFF = WD_OFF + LANES
GD_OFF = AD_OFF + LANES
ZR_W = GD_OFF + 2 * LANES
PROJ_TN = 512
ZR_TILES = ZR_W // PROJ_TN
RKV_TILES = 3 * RWKV_W // PROJ_TN
MOBA_TILES = MOBA_W // PROJ_TN
PROJ_PAD_W = ZR_W + 3 * MOBA_W
PROJ_TILES = PROJ_PAD_W // PROJ_TN
PROJ_VMEM_LIMIT_BYTES = 56 * 1024 * 1024

CHUNK = 64
RWKV_TB = 256
N_PAIRS = RWKV_W // LANES

NT = (((1,), (1,)), ((), ()))


def _cparams(*sem, vmem=VMEM_LIMIT_BYTES):
    return pltpu.CompilerParams(dimension_semantics=sem, vmem_limit_bytes=vmem)


def _mm(a, b):
    return jnp.dot(a.astype(BF16), b.astype(BF16), preferred_element_type=F32)


def _mm_nt(a, b):
    return lax.dot_general(a.astype(BF16), b.astype(BF16), NT, preferred_element_type=F32)


def _split2(x):
    hi = x.astype(BF16)
    lo = (x - hi.astype(F32)).astype(BF16)
    return hi, lo


def _mm3_nt(a, b):
    ah, al = _split2(a)
    bh, bl = _split2(b)
    d = lambda u, v: lax.dot_general(u, v, NT, preferred_element_type=F32)
    return d(ah, bh) + (d(ah, bl) + d(al, bh))


def _mm_exact_rhs(x, e):
    h0 = x.astype(BF16)
    r0 = x - h0.astype(F32)
    h1 = r0.astype(BF16)
    h2 = (r0 - h1.astype(F32)).astype(BF16)
    d = lambda u: jnp.dot(u, e, preferred_element_type=F32)
    return d(h0) + (d(h1) + d(h2))


def _mm_exact_lhs(e, x):
    h0 = x.astype(BF16)
    r0 = x - h0.astype(F32)
    h1 = r0.astype(BF16)
    h2 = (r0 - h1.astype(F32)).astype(BF16)
    d = lambda u: jnp.dot(e, u, preferred_element_type=F32)
    return d(h0) + (d(h1) + d(h2))


def _rms(x, g):
    ms = jnp.mean(x * x, axis=-1, keepdims=True)
    return (x * lax.rsqrt(ms + RMS_EPS)) * g


def _proj_kernel(x_ref, g_ref, wa_ref, wb_ref, wc_ref, *refs, paged, tm):
    if paged:
        zr_ref, q_ref, kp_ref, vp_ref, kb_ref, vb_ref, xn_ref = refs
    else:
        zr_ref, q_ref, k_ref, v_ref, xn_ref = refs
    j = pl.program_id(1)

    @pl.when(j == 0)
    def _():
        xn_ref[...] = _rms(x_ref[...], g_ref[...]).astype(BF16)

    tile = lambda w_ref: jnp.dot(xn_ref[...], w_ref[...], preferred_element_type=F32)
    heads_per_tile = PROJ_TN // MOBA_HD
    q0, k0, v0 = ZR_TILES, ZR_TILES + MOBA_TILES, ZR_TILES + 2 * MOBA_TILES

    @pl.when(j < RKV_TILES)
    def _():
        zr_ref[...] = tile(wa_ref)

    @pl.when(j == RKV_TILES)
    def _():
        zr_ref[...] = tile(wb_ref)

    if not paged:
        @pl.when((j >= q0) & (j < k0))
        def _():
            q_ref[...] = tile(wc_ref)

        @pl.when((j >= k0) & (j < v0))
        def _():
            k_ref[...] = tile(wc_ref)

        @pl.when(j >= v0)
        def _():
            v_ref[...] = tile(wc_ref)
        return

    @pl.when((j >= q0) & (j < k0))
    def _():
        acc = tile(wc_ref)
        for hh in range(heads_per_tile):
            q_ref[hh] = acc[:, hh * MOBA_HD:(hh + 1) * MOBA_HD]

    def store_kv(page_ref, bf_ref):
        acc = tile(wc_ref)
        for hh in range(heads_per_tile):
            blk = acc[:, hh * MOBA_HD:(hh + 1) * MOBA_HD]
            bf_ref[hh] = blk.astype(BF16)
            for pg in range(tm // PAGE_SIZE):
                page_ref[pg, hh] = blk[pg * PAGE_SIZE:(pg + 1) * PAGE_SIZE]

    @pl.when((j >= k0) & (j < v0))
    def _():
        store_kv(kp_ref, kb_ref)

    @pl.when(j >= v0)
    def _():
        store_kv(vp_ref, vb_ref)


def _proj(x2, g, w_rkv, w_lora, w_moba, *, paged):
    m = x2.shape[0]
    tm = 1024 if paged else m
    assert m % tm == 0
    q0, k0, v0 = ZR_TILES, ZR_TILES + MOBA_TILES, ZR_TILES + 2 * MOBA_TILES
    hpt = PROJ_TN // MOBA_HD
    clip = lambda j, lo: jnp.clip(j - lo, 0, MOBA_TILES - 1)
    in_specs = [
        pl.BlockSpec((tm, D_MODEL), lambda i, j: (i, 0)),
        pl.BlockSpec((1, D_MODEL), lambda i, j: (0, 0)),
        pl.BlockSpec((D_MODEL, PROJ_TN), lambda i, j: (0, jnp.minimum(j, RKV_TILES - 1))),
        pl.BlockSpec((D_MODEL, PROJ_TN), lambda i, j: (0, 0)),
        pl.BlockSpec((D_MODEL, PROJ_TN), lambda i, j: (0, jnp.clip(j - q0, 0, 3 * MOBA_TILES - 1))),
    ]
    zr_spec = pl.BlockSpec((tm, PROJ_TN), lambda i, j: (i, jnp.minimum(j, ZR_TILES - 1)))
    zr_shape = jax.ShapeDtypeStruct((m, ZR_W), F32)
    if paged:
        npg = m // PAGE_SIZE
        head_spec = lambda lo: pl.BlockSpec((hpt, tm, MOBA_HD), lambda i, j: (clip(j, lo), i, 0))
        page_spec = lambda lo: pl.BlockSpec((tm // PAGE_SIZE, hpt, PAGE_SIZE, MOBA_HD),
                                            lambda i, j: (i, clip(j, lo), 0, 0))
        out_specs = [zr_spec, head_spec(q0), page_spec(k0), page_spec(v0), head_spec(k0), head_spec(v0)]
        out_shape = [zr_shape,
                     jax.ShapeDtypeStruct((MOBA_H, m, MOBA_HD), F32),
                     jax.ShapeDtypeStruct((npg, MOBA_H, PAGE_SIZE, MOBA_HD), F32),
                     jax.ShapeDtypeStruct((npg, MOBA_H, PAGE_SIZE, MOBA_HD), F32),
                     jax.ShapeDtypeStruct((MOBA_H, m, MOBA_HD), BF16),
                     jax.ShapeDtypeStruct((MOBA_H, m, MOBA_HD), BF16)]
    else:
        flat_spec = lambda lo: pl.BlockSpec((tm, PROJ_TN), lambda i, j: (i, clip(j, lo)))
        out_specs = [zr_spec, flat_spec(q0), flat_spec(k0), flat_spec(v0)]
        out_shape = [zr_shape] + [jax.ShapeDtypeStruct((m, MOBA_W), F32)] * 3
    return pl.pallas_call(
        functools.partial(_proj_kernel, paged=paged, tm=tm),
        grid=(m // tm, PROJ_TILES),
        in_specs=in_specs, out_specs=out_specs, out_shape=out_shape,
        scratch_shapes=[pltpu.VMEM((tm, D_MODEL), BF16)],
        compiler_params=_cparams("parallel", "arbitrary", vmem=PROJ_VMEM_LIMIT_BYTES),
        name="proj_paged" if paged else "proj_flat",
    )(x2, g, w_rkv, w_lora, w_moba)


def _segsum(x, e):
    outs = [_mm_exact_rhs(x[:, g * LANES:(g + 1) * LANES], e) for g in range(x.shape[1] // LANES)]
    return jnp.concatenate(outs, axis=1)


def _softplus(x):
    return jnp.maximum(x, 0.0) + jnp.log1p(jnp.exp(-jnp.abs(x)))


def _sigmoid(x):
    return 1.0 / (1.0 + jnp.exp(-x))


def _rwkv_pre(z, zprev, prm, e):
    mu, w0, a0, k_k, k_a, wl, al, gl = prm
    zs = z + mu * (zprev - z)
    r = zs[:, 0:RWKV_W]
    k = zs[:, RWKV_W:2 * RWKV_W]
    v = zs[:, 2 * RWKV_W:3 * RWKV_W]
    wd = zs[:, WD_OFF:AD_OFF]
    ad = zs[:, AD_OFF:GD_OFF]
    gd = zs[:, GD_OFF:ZR_W]
    w_log = -_softplus(-(w0 + _mm(jnp.tanh(wd), wl))) - 0.5
    lw = -jnp.exp(w_log)
    a = _sigmoid(a0 + _mm(ad, al))
    g = _mm(_sigmoid(gd), gl)
    kk = k * k_k
    kk = kk * lax.rsqrt(jnp.maximum(_segsum(kk * kk, e), 1e-24))
    k2 = k * (1.0 + (a - 1.0) * k_a)
    b = kk * a
    return r, lw, k2, v, kk, b, g


def _rwkv_post(o, r, k2, v, g, r_k, lnx_w, lnx_b, e):
    inv_hd = 1.0 / RWKV_HD
    mean = _segsum(o, e) * inv_hd
    d = o - mean
    var = _segsum(d * d, e) * inv_hd
    on = d * lax.rsqrt(var + GN_EPS) * lnx_w + lnx_b
    on = on + _segsum(r * k2 * r_k, e) * v
    return on * g


def _rwkv_prompt_kernel(z_ref, mu_ref, w0_ref, a0_ref, kk_ref, ka_ref, rk_ref, lw_ref, lb_ref,
                        wl_ref, al_ref, gl_ref, e_ref, tri_ref,
                        o_ref, s_out_ref,
                        carry_ref, s_ref, r_s, lw_s, k_s, v_s, kk_s, b_s, o_s):
    i = pl.program_id(0)
    tb = z_ref.shape[0]

    @pl.when(i == 0)
    def _():
        carry_ref[...] = jnp.zeros_like(carry_ref)
        s_ref[...] = jnp.zeros_like(s_ref)

    e = e_ref[...]
    z = z_ref[...]
    row = lax.broadcasted_iota(jnp.int32, (tb, 1), 0)
    zprev = jnp.where(row == 0, carry_ref[...], pltpu.roll(z, 1, axis=0))
    carry_ref[...] = z[tb - 1:tb, :]
    prm = (mu_ref[...], w0_ref[...], a0_ref[...], kk_ref[...], ka_ref[...],
           wl_ref[...], al_ref[...], gl_ref[...])
    r, lw, k2, v, kk, b, g = _rwkv_pre(z, zprev, prm, e)
    for p in range(N_PAIRS):
        sl = slice(p * LANES, (p + 1) * LANES)
        r_s[p] = r[:, sl]
        lw_s[p] = lw[:, sl]
        k_s[p] = k2[:, sl]
        v_s[p] = v[:, sl]
        kk_s[p] = kk[:, sl]
        b_s[p] = b[:, sl]

    lane = lax.broadcasted_iota(jnp.int32, (1, LANES), 1)
    m0 = (lane < RWKV_HD).astype(F32)
    m1 = 1.0 - m0
    ri = lax.broadcasted_iota(jnp.int32, (LANES, LANES), 0)
    ci = lax.broadcasted_iota(jnp.int32, (LANES, LANES), 1)
    rim, cim = ri & (CHUNK - 1), ci & (CHUNK - 1)
    strict = rim > cim
    incl = rim >= cim
    eye = (ri == ci).astype(F32)
    tri = tri_ref[...]

    def stack(x):
        return jnp.concatenate([x * m0, x * m1], axis=0)

    pairs = range(N_PAIRS)

    def chunk_body(c, carry):
        rows = pl.ds(pl.multiple_of(c * CHUNK, CHUNK), CHUNK)
        cum = [_mm_exact_lhs(tri, lw_s[p, rows, :]) for p in pairs]
        ar, bk, bkh, v_st, dec = [], [], [], [], []
        for p in pairs:
            lw_ = lw_s[p, rows, :]
            k_ = k_s[p, rows, :]
            b_ = b_s[p, rows, :]
            tot = cum[p][CHUNK - 1:CHUNK, :]
            gi = jnp.exp(-cum[p])
            gt = jnp.exp(tot - cum[p])
            a_st = stack(-kk_s[p, rows, :] * jnp.exp(cum[p] - lw_))
            r_st = stack(r_s[p, rows, :] * jnp.exp(cum[p]))
            ar.append(jnp.concatenate([a_st, r_st], axis=0).astype(BF16))
            bk.append(jnp.concatenate([stack(b_ * gi), stack(k_ * gi)], axis=0).astype(BF16))
            bkh.append(jnp.concatenate([stack(b_ * gt), stack(k_ * gt)], axis=0).astype(BF16))
            v_st.append(stack(v_s[p, rows, :]))
            dec.append(jnp.exp(tot))
        gm = [_mm_nt(ar[p], bk[p]) for p in pairs]
        ah_rh = [_mm_nt(ar[p], s_ref[p]) for p in pairs]
        n = [jnp.where(strict, gm[p][:LANES, :LANES], 0.0) for p in pairs]
        akm = [jnp.where(strict, gm[p][:LANES, LANES:], 0.0) for p in pairs]
        rbk = [jnp.concatenate([jnp.where(incl, gm[p][LANES:, :LANES], 0.0),
                                jnp.where(incl, gm[p][LANES:, LANES:], 0.0)], axis=1).astype(BF16)
               for p in pairs]
        w = [_mm(akm[p], v_st[p]) for p in pairs]
        tinv = [eye + n[p] for p in pairs]
        pw = n
        for _ in range(5):
            pw = [_mm(pw[p], pw[p]) for p in pairs]
            tinv = [tinv[p] + _mm(tinv[p], pw[p]) for p in pairs]
        u_st = [_mm(tinv[p], ah_rh[p][:LANES] + w[p]) for p in pairs]
        uv = [jnp.concatenate([u_st[p], v_st[p]], axis=0).astype(BF16) for p in pairs]
        o_st = [ah_rh[p][LANES:] + _mm(rbk[p], uv[p]) for p in pairs]
        for p in pairs:
            o_s[p, rows, :] = o_st[p][:CHUNK] + o_st[p][CHUNK:]
        uv_t = [jnp.concatenate([u_st[p].T, v_st[p].T], axis=1) for p in pairs]
        s_new = [s_ref[p] * dec[p] + _mm(uv_t[p], bkh[p]) for p in pairs]
        for p in pairs:
            s_ref[p] = s_new[p]
        return carry

    lax.fori_loop(0, tb // CHUNK, chunk_body, 0)

    o = jnp.concatenate([o_s[p] for p in range(N_PAIRS)], axis=1)
    out = _rwkv_post(o, r, k2, v, g, rk_ref[...], lw_ref[...], lb_ref[...], e)
    o_ref[...] = out.astype(o_ref.dtype)

    @pl.when(i == pl.num_programs(0) - 1)
    def _():
        s_out_ref[...] = s_ref[...]


def _full(shape):
    nd = len(shape)
    return pl.BlockSpec(shape, lambda *_: (0,) * nd)


def _rwkv_prompt(zr, rp):
    t = zr.shape[0]
    tb = RWKV_TB
    assert t % tb == 0 and tb % CHUNK == 0
    vec = lambda n: _full((1, n))
    in_specs = [pl.BlockSpec((tb, ZR_W), lambda i: (i, 0)),
                vec(ZR_W), vec(RWKV_W), vec(RWKV_W), vec(RWKV_W), vec(RWKV_W), vec(RWKV_W),
                vec(RWKV_W), vec(RWKV_W),
                _full((LANES, RWKV_W)), _full((LANES, RWKV_W)), _full((2 * LANES, RWKV_W)),
                _full((LANES, LANES)), _full((CHUNK, CHUNK))]
    pair_buf = pltpu.VMEM((N_PAIRS, tb, LANES), F32)
    return pl.pallas_call(
        _rwkv_prompt_kernel,
        grid=(t // tb,),
        in_specs=in_specs,
        out_specs=[pl.BlockSpec((tb, RWKV_W), lambda i: (i, 0)), _full((N_PAIRS, LANES, LANES))],
        out_shape=[jax.ShapeDtypeStruct((t, RWKV_W), BF16),
                   jax.ShapeDtypeStruct((N_PAIRS, LANES, LANES), F32)],
        scratch_shapes=[pltpu.VMEM((1, ZR_W), F32), pltpu.VMEM((N_PAIRS, LANES, LANES), F32)]
                       + [pair_buf] * 7,
        compiler_params=_cparams("arbitrary"),
        name="rwkv_prompt",
    )(zr, rp["mu"], rp["w0"], rp["a0"], rp["k_k"], rp["k_a"], rp["r_k"], rp["lnx_w"], rp["lnx_b"],
      rp["wl"], rp["al"], rp["gl"], rp["e"], rp["tri"])


def _rwkv_sample_pre_kernel(z_ref, prev_ref, mu_ref, w0_ref, a0_ref, kk_ref, ka_ref,
                            wl_ref, al_ref, gl_ref, e_ref,
                            r_o, w_o, k_o, v_o, kk_o, b_o, g_o):
    prm = (mu_ref[...], w0_ref[...], a0_ref[...], kk_ref[...], ka_ref[...],
           wl_ref[...], al_ref[...], gl_ref[...])
    r, lw, k2, v, kk, b, g = _rwkv_pre(z_ref[...], prev_ref[...], prm, e_ref[...])
    r_o[...] = r
    w_o[...] = jnp.exp(lw)
    k_o[...] = k2
    v_o[...] = v
    kk_o[...] = kk
    b_o[...] = b
    g_o[...] = g


def _rwkv_step_kernel(s_ref, r_ref, w_ref, k_ref, kk_ref, b_ref, v_ref, s_out, o_out):
    s = s_ref[0]
    sa = jnp.sum(s * (-kk_ref[0]), axis=-1, keepdims=True)
    s_new = s * w_ref[0] + sa * b_ref[0] + v_ref[0] * k_ref[0]
    s_out[0] = s_new
    o_out[0] = jnp.sum(s_new * r_ref[0], axis=-1, keepdims=True)


def _rwkv_sample_post_kernel(o_ref, r_ref, k_ref, v_ref, g_ref, rk_ref, lw_ref, lb_ref, e_ref, out_ref):
    out = _rwkv_post(o_ref[...], r_ref[...], k_ref[...], v_ref[...], g_ref[...],
                     rk_ref[...], lw_ref[...], lb_ref[...], e_ref[...])
    out_ref[...] = out.astype(out_ref.dtype)


def _rwkv_sample(zr, prev_pad, s0, rp):
    n = zr.shape[0]
    vec = lambda w: _full((1, w))
    rows = jax.ShapeDtypeStruct((n, RWKV_W), F32)
    r, w, k2, v, kk, b, g = pl.pallas_call(
        _rwkv_sample_pre_kernel,
        grid=(1,),
        in_specs=[_full((n, ZR_W)), _full((n, ZR_W)), vec(ZR_W), vec(RWKV_W), vec(RWKV_W), vec(RWKV_W),
                  vec(RWKV_W), _full((LANES, RWKV_W)), _full((LANES, RWKV_W)), _full((2 * LANES, RWKV_W)),
                  _full((LANES, LANES))],
        out_specs=[_full((n, RWKV_W))] * 7,
        out_shape=[rows] * 7,
        compiler_params=_cparams("arbitrary"),
        name="rwkv_sample_pre",
    )(zr, prev_pad, rp["mu"], rp["w0"], rp["a0"], rp["k_k"], rp["k_a"], rp["wl"], rp["al"], rp["gl"], rp["e"])

    as_row = lambda u: u.reshape(n, RWKV_H, 1, RWKV_HD)
    as_col = lambda u: u.reshape(n, RWKV_H, RWKV_HD, 1)
    row_spec = pl.BlockSpec((1, RWKV_H, 1, RWKV_HD), lambda i: (i, 0, 0, 0))
    col_spec = pl.BlockSpec((1, RWKV_H, RWKV_HD, 1), lambda i: (i, 0, 0, 0))
    st_spec = pl.BlockSpec((1, RWKV_H, RWKV_HD, RWKV_HD), lambda i: (i, 0, 0, 0))
    s_new, o = pl.pallas_call(
        _rwkv_step_kernel,
        grid=(n,),
        in_specs=[st_spec] + [row_spec] * 5 + [col_spec],
        out_specs=[st_spec, col_spec],
        out_shape=[jax.ShapeDtypeStruct((n, RWKV_H, RWKV_HD, RWKV_HD), F32),
                   jax.ShapeDtypeStruct((n, RWKV_H, RWKV_HD, 1), F32)],
        compiler_params=_cparams("parallel"),
        name="rwkv_sample_step",
    )(s0, as_row(r), as_row(w), as_row(k2), as_row(kk), as_row(b), as_col(v))

    out = pl.pallas_call(
        _rwkv_sample_post_kernel,
        grid=(1,),
        in_specs=[_full((n, RWKV_W))] * 5 + [vec(RWKV_W)] * 3 + [_full((LANES, LANES))],
        out_specs=_full((n, RWKV_W)),
        out_shape=jax.ShapeDtypeStruct((n, RWKV_W), BF16),
        compiler_params=_cparams("arbitrary"),
        name="rwkv_sample_post",
    )(o.reshape(n, RWKV_W), r, k2, v, g, rp["r_k"], rp["lnx_w"], rp["lnx_b"], rp["e"])
    return out, s_new


def _block_mean_kernel(k_ref, km_ref):
    x = k_ref[...]
    npg = x.shape[0]
    ppb = MOBA_BLOCK // PAGE_SIZE
    x = x.reshape(npg // ppb, MOBA_BLOCK, MOBA_HD)
    km_ref[0] = jnp.sum(x, axis=1) * (1.0 / MOBA_BLOCK)


def _block_means(k_pages):
    npg = k_pages.shape[0]
    nb = npg * PAGE_SIZE // MOBA_BLOCK
    return pl.pallas_call(
        _block_mean_kernel,
        grid=(MOBA_H,),
        in_specs=[pl.BlockSpec((npg, 1, PAGE_SIZE, MOBA_HD), lambda h: (0, h, 0, 0))],
        out_specs=pl.BlockSpec((1, nb, MOBA_HD), lambda h: (h, 0, 0)),
        out_shape=jax.ShapeDtypeStruct((MOBA_H, nb, MOBA_HD), F32),
        compiler_params=_cparams("parallel"),
        name="moba_block_means",
    )(k_pages)


MOBA_GROUP = 4
MASK_BIAS = -1e30
LOG2E = 1.4426950408889634
MOBA_TILES_PER_STEP = 2
MOBA_VMEM_LIMIT_BYTES = 56 * 1024 * 1024


def _select_bias_t(gate_t, n_valid):
    neg = jnp.float32(-jnp.inf)
    blk = lax.broadcasted_iota(jnp.int32, gate_t.shape, 0).astype(F32)
    g = jnp.where(blk < n_valid, gate_t, neg)
    bias = jnp.full(gate_t.shape, MASK_BIAS, F32)
    for _ in range(MOBA_TOPK):
        mx = jnp.max(g, axis=0, keepdims=True)
        idx = jnp.min(jnp.where(g == mx, blk, jnp.float32(1e9)), axis=0, keepdims=True)
        pick = (blk == idx) & (mx > neg)
        bias = jnp.where(pick, 0.0, bias)
        g = jnp.where(pick, neg, g)
    return bias


def _moba_prompt_kernel(q_ref, km_ref, kb_ref, vb_ref, o_ref, qa_ref, acc_ref, s_ref):
    step = pl.program_id(1)
    ng, tq, _ = q_ref.shape
    nsub = tq // MOBA_BLOCK
    first = step * nsub
    scale = MOBA_HD ** -0.5
    neg = jnp.float32(-jnp.inf)
    rr = lax.broadcasted_iota(jnp.int32, (MOBA_BLOCK, MOBA_BLOCK), 0)
    cc = lax.broadcasted_iota(jnp.int32, (MOBA_BLOCK, MOBA_BLOCK), 1)
    causal = cc <= rr
    ones_v = jnp.ones((MOBA_BLOCK, MOBA_HD), BF16)
    qcol = lax.broadcasted_iota(jnp.int32, (1, tq), 1)
    n_valid = (first + qcol // MOBA_BLOCK).astype(F32)

    heads = range(ng)
    lane_blk = lax.broadcasted_iota(jnp.int32, (MOBA_BLOCK, LANES), 1)

    def scores(g, j):
        st = pl.multiple_of(j * MOBA_BLOCK, MOBA_BLOCK)
        onehot = (lane_blk == j).astype(BF16)
        kj = jnp.concatenate([kb_ref[g, pl.ds(st, MOBA_BLOCK), :], onehot], axis=1)
        return lax.dot_general(qa_ref[g], kj, NT, preferred_element_type=F32)

    def values(g, j):
        st = pl.multiple_of(j * MOBA_BLOCK, MOBA_BLOCK)
        return jnp.concatenate([vb_ref[g, pl.ds(st, MOBA_BLOCK), :], ones_v], axis=1)

    m0 = []
    for g in heads:
        q = q_ref[g]
        bias = _select_bias_t(_mm3_nt(km_ref[g], q), n_valid).T
        qs = (q * (scale * LOG2E)).astype(BF16)
        qa_ref[g] = jnp.concatenate([qs, bias.astype(BF16)], axis=1)
        m_sub = []
        for u in range(nsub):
            rows = slice(u * MOBA_BLOCK, (u + 1) * MOBA_BLOCK)
            st = pl.multiple_of((first + u) * MOBA_BLOCK, MOBA_BLOCK)
            kd = kb_ref[g, pl.ds(st, MOBA_BLOCK), :]
            s = jnp.where(causal, lax.dot_general(qs[rows], kd, NT, preferred_element_type=F32), neg)
            m = jnp.max(s, axis=1, keepdims=True)
            p = jnp.exp2(s - m).astype(BF16)
            m_sub.append(m)
            acc_ref[g, rows, :] = jnp.dot(p, values(g, first + u), preferred_element_type=F32)
        m0.append(jnp.concatenate(m_sub, axis=0))
    for g in heads:
        s_ref[0, g] = scores(g, 0)

    def body(j, ms):
        slot = j & 1
        s_next = [scores(g, j + 1) for g in heads]
        s = [s_ref[slot, g] for g in heads]
        m_new = [jnp.maximum(ms[g], jnp.max(s[g], axis=1, keepdims=True)) for g in heads]
        p = [jnp.exp2(s[g] - m_new[g]).astype(BF16) for g in heads]
        pv = [jnp.dot(p[g], values(g, j), preferred_element_type=F32) for g in heads]
        for g in heads:
            acc_ref[g] = jnp.exp2(ms[g] - m_new[g]) * acc_ref[g] + pv[g]
            s_ref[1 - slot, g] = s_next[g]
        return tuple(m_new)

    lax.fori_loop(0, first + nsub - 1, body, tuple(m0))
    for g in heads:
        acc = acc_ref[g]
        o_ref[:, g * MOBA_HD:(g + 1) * MOBA_HD] = (acc[:, :MOBA_HD] / acc[:, MOBA_HD:]).astype(o_ref.dtype)


def _moba_prompt(q3, km, kb, vb):
    t = q3.shape[1]
    nb = t // MOBA_BLOCK
    tq = MOBA_TILES_PER_STEP * MOBA_BLOCK
    assert t % tq == 0 and nb <= LANES and MOBA_H % MOBA_GROUP == 0
    km_pad = jnp.concatenate([km, jnp.zeros((MOBA_H, LANES - nb, MOBA_HD), F32)], axis=1)
    ng = MOBA_GROUP
    resident = lambda: pl.BlockSpec((ng, t, MOBA_HD), lambda h, i: (h, 0, 0), pipeline_mode=pl.Buffered(1))
    return pl.pallas_call(
        _moba_prompt_kernel,
        grid=(MOBA_H // ng, t // tq),
        in_specs=[pl.BlockSpec((ng, tq, MOBA_HD), lambda h, i: (h, i, 0)),
                  pl.BlockSpec((ng, LANES, MOBA_HD), lambda h, i: (h, 0, 0)),
                  resident(), resident()],
        out_specs=pl.BlockSpec((tq, ng * MOBA_HD), lambda h, i: (i, h)),
        out_shape=jax.ShapeDtypeStruct((t, MOBA_W), BF16),
        scratch_shapes=[pltpu.VMEM((ng, tq, 2 * MOBA_HD), BF16),
                        pltpu.VMEM((ng, tq, 2 * MOBA_HD), F32),
                        pltpu.VMEM((2, ng, tq, MOBA_BLOCK), F32)],
        compiler_params=_cparams("parallel", "arbitrary", vmem=MOBA_VMEM_LIMIT_BYTES),
        name="moba_prompt",
    )(q3, km_pad, kb, vb)


PAGES_PER_STEP = 8


def _page_sum_kernel(pt_ref, *refs):
    del pt_ref
    k_refs, out_ref = refs[:PAGES_PER_STEP], refs[PAGES_PER_STEP]
    for h in range(MOBA_H):
        rows = [jnp.sum(k_refs[u][0, 0, h], axis=0, keepdims=True) for u in range(PAGES_PER_STEP)]
        out_ref[0, h] = jnp.concatenate(rows, axis=0)


def _page_sums(cache_k, page_table):
    nbt, n_pages = page_table.shape
    assert n_pages % PAGES_PER_STEP == 0
    spec = lambda u: pl.BlockSpec((1, 1, MOBA_H, PAGE_SIZE, MOBA_HD),
                                  lambda b, g, pt: (pt[b, g * PAGES_PER_STEP + u], 0, 0, 0, 0))
    return pl.pallas_call(
        _page_sum_kernel,
        grid_spec=pltpu.PrefetchScalarGridSpec(
            num_scalar_prefetch=1,
            grid=(nbt, n_pages // PAGES_PER_STEP),
            in_specs=[spec(u) for u in range(PAGES_PER_STEP)],
            out_specs=pl.BlockSpec((1, MOBA_H, PAGES_PER_STEP, MOBA_HD), lambda b, g, pt: (b, 0, g, 0))),
        out_shape=jax.ShapeDtypeStruct((nbt, MOBA_H, n_pages, MOBA_HD), F32),
        compiler_params=_cparams("parallel", "arbitrary"),
        name="moba_page_sums",
    )(page_table, *([cache_k] * PAGES_PER_STEP))


def _sample_gate_kernel(q_ref, ps_ref, sel_ref):
    n_pages = ps_ref.shape[2]
    lane = lax.broadcasted_iota(jnp.int32, (1, n_pages), 1)
    even = (lane & 1) == 0
    neg = jnp.float32(-jnp.inf)
    big = jnp.int32(2 ** 30)
    out_lane = lax.broadcasted_iota(jnp.int32, (1, LANES), 1)
    rows = []
    for h in range(MOBA_H):
        qh = jnp.broadcast_to(q_ref[0, h:h + 1, :], (8, MOBA_HD))
        gp = _mm3_nt(qh, ps_ref[0, h])[0:1, :]
        nbr = jnp.where(even, pltpu.roll(gp, n_pages - 1, axis=1), pltpu.roll(gp, 1, axis=1))
        g = jnp.where(even, (gp + nbr) * (1.0 / MOBA_BLOCK), neg)
        res = jnp.zeros((1, LANES), jnp.int32)
        for s in range(MOBA_TOPK):
            mx = jnp.max(g, axis=1, keepdims=True)
            idx = jnp.min(jnp.where(g == mx, lane, big), axis=1, keepdims=True)
            g = jnp.where(lane == idx, neg, g)
            res = jnp.where(out_lane == s, idx >> 1, res)
        rows.append(res)
    sel_ref[0] = jnp.concatenate(rows, axis=0)


def _sample_gate(q3, psums):
    nbt, _, n_pages, _ = psums.shape
    assert n_pages == LANES and n_pages // 2 >= MOBA_TOPK
    return pl.pallas_call(
        _sample_gate_kernel,
        grid=(nbt,),
        in_specs=[pl.BlockSpec((1, MOBA_H, MOBA_HD), lambda b: (b, 0, 0)),
                  pl.BlockSpec((1, MOBA_H, n_pages, MOBA_HD), lambda b: (b, 0, 0, 0))],
        out_specs=pl.BlockSpec((1, MOBA_H, LANES), lambda b: (b, 0, 0)),
        out_shape=jax.ShapeDtypeStruct((nbt, MOBA_H, LANES), jnp.int32),
        compiler_params=_cparams("parallel"),
        name="moba_sample_gate",
    )(q3, psums)


SAMPLE_PAGES = MOBA_TOPK * (MOBA_BLOCK // PAGE_SIZE)


def _sample_attn_kernel(ph_ref, q_ref, kn_ref, vn_ref, *refs):
    del ph_ref
    n_in = MOBA_H * SAMPLE_PAGES
    k_refs, v_refs, o_ref = refs[:n_in], refs[n_in:2 * n_in], refs[2 * n_in]
    scale = MOBA_HD ** -0.5
    for h in range(MOBA_H):
        q = q_ref[0, h]
        kp = k_refs[h * SAMPLE_PAGES:(h + 1) * SAMPLE_PAGES]
        vp = v_refs[h * SAMPLE_PAGES:(h + 1) * SAMPLE_PAGES]
        s_self = jnp.sum(kn_ref[0, h] * q, axis=1, keepdims=True) * scale
        ss = [jnp.sum(kp[u][0, 0, 0] * q, axis=1, keepdims=True) * scale for u in range(SAMPLE_PAGES)]
        m = s_self
        for s in ss:
            m = jnp.maximum(m, jnp.max(s, axis=0, keepdims=True))
        p_self = jnp.exp(s_self - m)
        l = p_self
        acc = p_self * vn_ref[0, h]
        for u in range(SAMPLE_PAGES):
            p = jnp.exp(ss[u] - m)
            l = l + jnp.sum(p, axis=0, keepdims=True)
            acc = acc + jnp.sum(p * vp[u][0, 0, 0], axis=0, keepdims=True)
        o_ref[0, h] = acc / l


def _sample_attn(q4, kn4, vn4, cache_k, cache_v, phys):
    nbt = q4.shape[0]
    vec_spec = pl.BlockSpec((1, MOBA_H, 1, MOBA_HD), lambda b, ph: (b, 0, 0, 0))
    page_spec = lambda h, u: pl.BlockSpec(
        (1, 1, 1, PAGE_SIZE, MOBA_HD),
        lambda b, ph: (ph[(b * MOBA_H + h) * SAMPLE_PAGES + u], 0, h, 0, 0))
    pages = [page_spec(h, u) for h in range(MOBA_H) for u in range(SAMPLE_PAGES)]
    return pl.pallas_call(
        _sample_attn_kernel,
        grid_spec=pltpu.PrefetchScalarGridSpec(
            num_scalar_prefetch=1,
            grid=(nbt,),
            in_specs=[vec_spec, vec_spec, vec_spec] + pages + pages,
            out_specs=vec_spec),
        out_shape=jax.ShapeDtypeStruct((nbt, MOBA_H, 1, MOBA_HD), F32),
        compiler_params=_cparams("arbitrary"),
        name="moba_sample_attn",
    )(phys, q4, kn4, vn4, *([cache_k] * len(pages)), *([cache_v] * len(pages)))


def _outproj_kernel(x_ref, or_ref, om_ref, wr_ref, wm_ref, g_ref, o_ref):
    y = (jnp.dot(or_ref[...], wr_ref[...], preferred_element_type=F32)
         + jnp.dot(om_ref[...], wm_ref[...], preferred_element_type=F32))
    o_ref[...] = x_ref[...] + _rms(y, g_ref[...])


def _outproj(x2, o_r, o_m, w_r, w_m, g):
    m = x2.shape[0]
    tm = min(m, 512)
    assert m % tm == 0
    return pl.pallas_call(
        _outproj_kernel,
        grid=(m // tm,),
        in_specs=[pl.BlockSpec((tm, D_MODEL), lambda i: (i, 0)),
                  pl.BlockSpec((tm, RWKV_W), lambda i: (i, 0)),
                  pl.BlockSpec((tm, MOBA_W), lambda i: (i, 0)),
                  _full((RWKV_W, D_MODEL)), _full((MOBA_W, D_MODEL)), _full((1, D_MODEL))],
        out_specs=pl.BlockSpec((tm, D_MODEL), lambda i: (i, 0)),
        out_shape=jax.ShapeDtypeStruct((m, D_MODEL), F32),
        compiler_params=_cparams("parallel"),
        name="outproj",
    )(x2, o_r, o_m, w_r, w_m, g)


MLP_TF = 1024
MLP_TM = 512


def _mlp_kernel(x_ref, g3_ref, g4_ref, wu_ref, wd_ref, o_ref, xn_ref, acc_ref):
    j = pl.program_id(1)

    @pl.when(j == 0)
    def _():
        xn_ref[...] = _rms(x_ref[...], g3_ref[...]).astype(BF16)
        acc_ref[...] = jnp.zeros_like(acc_ref)

    h = jnp.dot(xn_ref[...], wu_ref[...], preferred_element_type=F32)
    h = jnp.square(jnp.maximum(h, 0.0))
    acc_ref[...] += jnp.dot(h.astype(BF16), wd_ref[...], preferred_element_type=F32)

    @pl.when(j == pl.num_programs(1) - 1)
    def _():
        o_ref[...] = x_ref[...] + _rms(acc_ref[...], g4_ref[...])


def _mlp(x1, g3, g4, w_up, w_down):
    m = x1.shape[0]
    tm = min(m, MLP_TM)
    assert m % tm == 0
    return pl.pallas_call(
        _mlp_kernel,
        grid=(m // tm, D_FF // MLP_TF),
        in_specs=[pl.BlockSpec((tm, D_MODEL), lambda i, j: (i, 0)),
                  pl.BlockSpec((1, D_MODEL), lambda i, j: (0, 0)),
                  pl.BlockSpec((1, D_MODEL), lambda i, j: (0, 0)),
                  pl.BlockSpec((D_MODEL, MLP_TF), lambda i, j: (0, j)),
                  pl.BlockSpec((MLP_TF, D_MODEL), lambda i, j: (j, 0))],
        out_specs=pl.BlockSpec((tm, D_MODEL), lambda i, j: (i, 0)),
        out_shape=jax.ShapeDtypeStruct((m, D_MODEL), F32),
        scratch_shapes=[pltpu.VMEM((tm, D_MODEL), BF16), pltpu.VMEM((tm, D_MODEL), F32)],
        compiler_params=_cparams("parallel", "arbitrary"),
        name="mlp",
    )(x1, g3, g4, w_up, w_down)


def _pad_cols(a, lora_only=False):
    z = lambda n: jnp.zeros(a.shape[:-1] + (n,), a.dtype)
    lo = 0 if lora_only else 3 * RWKV_W
    wd = a[..., lo:lo + DECAY_LORA]
    ad = a[..., lo + DECAY_LORA:lo + DECAY_LORA + AAA_LORA]
    gd = a[..., lo + DECAY_LORA + AAA_LORA:lo + DECAY_LORA + AAA_LORA + GATE_LORA]
    parts = [wd, z(LANES - DECAY_LORA), ad, z(LANES - AAA_LORA), gd, z(2 * LANES - GATE_LORA)]
    if not lora_only:
        parts.insert(0, a[..., :lo])
    return jnp.concatenate(parts, axis=-1)


def _unpad_cols(zr):
    return jnp.concatenate([zr[..., :3 * RWKV_W + DECAY_LORA], zr[..., AD_OFF:AD_OFF + AAA_LORA],
                            zr[..., GD_OFF:GD_OFF + GATE_LORA]], axis=-1)


def _pad_rows(a, n):
    return jnp.concatenate([a, jnp.zeros((n - a.shape[0],) + a.shape[1:], a.dtype)], axis=0)


def _unpair_state(s_bd):
    h0 = s_bd[:, :RWKV_HD, :RWKV_HD]
    h1 = s_bd[:, RWKV_HD:, RWKV_HD:]
    return jnp.stack([h0, h1], axis=1).reshape(RWKV_H, RWKV_HD, RWKV_HD)


def kernel(x_prompt, x_sample, cache_k, cache_v, page_table, state_wkv, state_shift, g_mix_pre, g_mix_post, g_ffn_pre, g_ffn_post, w_in, mu_shift, w0, w_lora2, a0, a_lora2, g_lora2, k_k, k_a, r_k, lnx_w, lnx_b, w_out, w_up, w_down):
    depth = w_in.shape[0]
    bsz, seq, _ = x_prompt.shape
    dbt, dseq, _ = x_sample.shape
    assert depth == 1 and bsz == 1 and dseq == 1, "kernels are written for the stated shapes"
    n_pages = page_table.shape[1]
    l = 0

    row = lambda a: a.reshape(1, -1)
    w_rkv = w_in[l, :, :3 * RWKV_W].astype(BF16)
    w_lora = _pad_cols(w_in[l, :, 3 * RWKV_W:SHIFT_W].astype(BF16), lora_only=True)
    w_moba = w_in[l, :, SHIFT_W:].astype(BF16)
    lane = jnp.arange(LANES)
    e_seg = ((lane[:, None] // RWKV_HD) == (lane[None, :] // RWKV_HD)).astype(BF16)
    ci = jnp.arange(CHUNK)
    tri = (ci[:, None] >= ci[None, :]).astype(BF16)
    rp = dict(mu=row(_pad_cols(mu_shift[l])), w0=row(w0[l]), a0=row(a0[l]), k_k=row(k_k[l]),
              k_a=row(k_a[l]), r_k=row(r_k[l]), lnx_w=row(lnx_w[l]), lnx_b=row(lnx_b[l]),
              wl=_pad_rows(w_lora2[l], LANES).astype(BF16), al=_pad_rows(a_lora2[l], LANES).astype(BF16),
              gl=_pad_rows(g_lora2[l], 2 * LANES).astype(BF16), e=e_seg, tri=tri)
    w_or = w_out[l, :RWKV_W].astype(BF16)
    w_om = w_out[l, RWKV_W:].astype(BF16)
    w_up_b = w_up[l].astype(BF16)
    w_down_b = w_down[l].astype(BF16)
    g1, g2, g3, g4 = row(g_mix_pre[l]), row(g_mix_post[l]), row(g_ffn_pre[l]), row(g_ffn_post[l])

    xp = x_prompt.reshape(seq, D_MODEL)
    zr, q3, k_pages, v_pages, kb, vb = _proj(xp, g1, w_rkv, w_lora, w_moba, paged=True)
    o_r, s_bd = _rwkv_prompt(zr, rp)
    km = _block_means(k_pages)
    o_m = _moba_prompt(q3, km, kb, vb)
    x1 = _outproj(xp, o_r, o_m, w_or, w_om, g2)
    y_prompt = _mlp(x1, g3, g4, w_up_b, w_down_b).reshape(bsz, seq, D_MODEL)
    npg = seq // PAGE_SIZE
    k_prompt = k_pages.reshape(bsz, npg, 1, MOBA_H, PAGE_SIZE, MOBA_HD)
    v_prompt = v_pages.reshape(bsz, npg, 1, MOBA_H, PAGE_SIZE, MOBA_HD)
    wkv_prompt = _unpair_state(s_bd).reshape(1, bsz, RWKV_H, RWKV_HD, RWKV_HD)
    shift_prompt = _unpad_cols(zr[seq - 1:seq]).reshape(1, bsz, SHIFT_W)

    xs = x_sample.reshape(dbt, D_MODEL)
    zr_s, q_s, k_s, v_s = _proj(xs, g1, w_rkv, w_lora, w_moba, paged=False)
    o_rs, wkv_s = _rwkv_sample(zr_s, _pad_cols(state_shift[l]), state_wkv[l], rp)
    psums = _page_sums(cache_k, page_table)
    sel = _sample_gate(q_s.reshape(dbt, MOBA_H, MOBA_HD), psums)[:, :, :MOBA_TOPK]
    ppb = MOBA_BLOCK // PAGE_SIZE
    logical = sel[..., None] * ppb + jnp.arange(ppb, dtype=jnp.int32)
    phys = jnp.take_along_axis(page_table, logical.reshape(dbt, -1), axis=1).reshape(-1)
    as4 = lambda u: u.reshape(dbt, MOBA_H, 1, MOBA_HD)
    o_ms = _sample_attn(as4(q_s), as4(k_s), as4(v_s), cache_k, cache_v, phys)
    o_ms = o_ms.reshape(dbt, MOBA_W).astype(BF16)
    x1s = _outproj(xs, o_rs, o_ms, w_or, w_om, g2)
    y_sample = _mlp(x1s, g3, g4, w_up_b, w_down_b).reshape(dbt, dseq, D_MODEL)
    k_sample = k_s.reshape(dbt, 1, MOBA_H, dseq, MOBA_HD)
    v_sample = v_s.reshape(dbt, 1, MOBA_H, dseq, MOBA_HD)
    wkv_sample = wkv_s.reshape(1, dbt, RWKV_H, RWKV_HD, RWKV_HD)
    shift_sample = _unpad_cols(zr_s).reshape(1, dbt, SHIFT_W)

    return (y_prompt, y_sample, k_prompt, v_prompt, wkv_prompt, shift_prompt,
            k_sample, v_sample, wkv_sample, shift_sample)
```

```python
import functools

import jax
import jax.numpy as jnp
from jax import lax
from jax.experimental import pallas as pl
from jax.experimental.pallas import tpu as pltpu

F32 = jnp.float32
BF16 = jnp.bfloat16

D_MODEL = 2048
RWKV_W = 1024
RWKV_HD = 64
RWKV_H = RWKV_W // RWKV_HD
DECAY_LORA = 64
AAA_LORA = 64
GATE_LORA = 160
MOBA_W = 1024
MOBA_HD = 128
MOBA_H = MOBA_W // MOBA_HD
MOBA_BLOCK = 256
MOBA_TOPK = 3
PAGE_SIZE = 128
D_FF = 4 * D_MODEL
RMS_EPS = 1e-6
GN_EPS = 64e-5
SHIFT_W = 3 * RWKV_W + DECAY_LORA + AAA_LORA + GATE_LORA

LANES = 128
VMEM_LIMIT_BYTES = 48 * 1024 * 1024

WD_OFF = 3 * RWKV_W
AD_OFF = WD_OFF + LANES
GD_OFF = AD_OFF + LANES
ZR_W = GD_OFF + 2 * LANES
PROJ_TN = 512
ZR_TILES = ZR_W // PROJ_TN
RKV_TILES = 3 * RWKV_W // PROJ_TN
MOBA_TILES = MOBA_W // PROJ_TN
PROJ_PAD_W = ZR_W + 3 * MOBA_W
PROJ_TILES = PROJ_PAD_W // PROJ_TN
PROJ_VMEM_LIMIT_BYTES = 56 * 1024 * 1024

CHUNK = 64
RWKV_TB = 256
N_PAIRS = RWKV_W // LANES

NT = (((1,), (1,)), ((), ()))


def _cparams(*sem, vmem=VMEM_LIMIT_BYTES):
    return pltpu.CompilerParams(dimension_semantics=sem, vmem_limit_bytes=vmem)


def _mm(a, b):
    return jnp.dot(a.astype(BF16), b.astype(BF16), preferred_element_type=F32)


def _mm_nt(a, b):
    return lax.dot_general(a.astype(BF16), b.astype(BF16), NT, preferred_element_type=F32)


def _split2(x):
    hi = x.astype(BF16)
    lo = (x - hi.astype(F32)).astype(BF16)
    return hi, lo


def _mm3_nt(a, b):
    ah, al = _split2(a)
    bh, bl = _split2(b)
    d = lambda u, v: lax.dot_general(u, v, NT, preferred_element_type=F32)
    return d(ah, bh) + (d(ah, bl) + d(al, bh))


def _mm_exact_lhs(e, x):
    h0 = x.astype(BF16)
    r0 = x - h0.astype(F32)
    h1 = r0.astype(BF16)
    h2 = (r0 - h1.astype(F32)).astype(BF16)
    d = lambda u: jnp.dot(e, u, preferred_element_type=F32)
    return d(h0) + (d(h1) + d(h2))


def _rms(x, g):
    ms = jnp.mean(x * x, axis=-1, keepdims=True)
    return (x * lax.rsqrt(ms + RMS_EPS)) * g


def _proj_kernel(x_ref, g_ref, wa_ref, wb_ref, wc_ref, *refs, paged, tm):
    if paged:
        zr_ref, q_ref, kp_ref, vp_ref, kb_ref, vb_ref, xn_ref = refs
    else:
        zr_ref, q_ref, k_ref, v_ref, xn_ref = refs
    j = pl.program_id(1)

    @pl.when(j == 0)
    def _():
        xn_ref[...] = _rms(x_ref[...], g_ref[...]).astype(BF16)

    tile = lambda w_ref: jnp.dot(xn_ref[...], w_ref[...], preferred_element_type=F32)
    heads_per_tile = PROJ_TN // MOBA_HD
    q0, k0, v0 = ZR_TILES, ZR_TILES + MOBA_TILES, ZR_TILES + 2 * MOBA_TILES

    @pl.when(j < RKV_TILES)
    def _():
        zr_ref[...] = tile(wa_ref)

    @pl.when(j == RKV_TILES)
    def _():
        zr_ref[...] = tile(wb_ref)

    if not paged:
        @pl.when((j >= q0) & (j < k0))
        def _():
            q_ref[...] = tile(wc_ref)

        @pl.when((j >= k0) & (j < v0))
        def _():
            k_ref[...] = tile(wc_ref)

        @pl.when(j >= v0)
        def _():
            v_ref[...] = tile(wc_ref)
        return

    @pl.when((j >= q0) & (j < k0))
    def _():
        acc = tile(wc_ref)
        for hh in range(heads_per_tile):
            q_ref[hh] = acc[:, hh * MOBA_HD:(hh + 1) * MOBA_HD]

    def store_kv(page_ref, bf_ref):
        acc = tile(wc_ref)
        for hh in range(heads_per_tile):
            blk = acc[:, hh * MOBA_HD:(hh + 1) * MOBA_HD]
            bf_ref[hh] = blk.astype(BF16)
            for pg in range(tm // PAGE_SIZE):
                page_ref[pg, hh] = blk[pg * PAGE_SIZE:(pg + 1) * PAGE_SIZE]

    @pl.when((j >= k0) & (j < v0))
    def _():
        store_kv(kp_ref, kb_ref)

    @pl.when(j >= v0)
    def _():
        store_kv(vp_ref, vb_ref)


def _proj(x2, g, w_rkv, w_lora, w_moba, *, paged):
    m = x2.shape[0]
    tm = 1024 if paged else m
    assert m % tm == 0
    q0, k0, v0 = ZR_TILES, ZR_TILES + MOBA_TILES, ZR_TILES + 2 * MOBA_TILES
    hpt = PROJ_TN // MOBA_HD
    clip = lambda j, lo: jnp.clip(j - lo, 0, MOBA_TILES - 1)
    in_specs = [
        pl.BlockSpec((tm, D_MODEL), lambda i, j: (i, 0)),
        pl.BlockSpec((1, D_MODEL), lambda i, j: (0, 0)),
        pl.BlockSpec((D_MODEL, PROJ_TN), lambda i, j: (0, jnp.minimum(j, RKV_TILES - 1))),
        pl.BlockSpec((D_MODEL, PROJ_TN), lambda i, j: (0, 0)),
        pl.BlockSpec((D_MODEL, PROJ_TN), lambda i, j: (0, jnp.clip(j - q0, 0, 3 * MOBA_TILES - 1))),
    ]
    zr_spec = pl.BlockSpec((tm, PROJ_TN), lambda i, j: (i, jnp.minimum(j, ZR_TILES - 1)))
    zr_shape = jax.ShapeDtypeStruct((m, ZR_W), F32)
    if paged:
        npg = m // PAGE_SIZE
        head_spec = lambda lo: pl.BlockSpec((hpt, tm, MOBA_HD), lambda i, j: (clip(j, lo), i, 0))
        page_spec = lambda lo: pl.BlockSpec((tm // PAGE_SIZE, hpt, PAGE_SIZE, MOBA_HD),
                                            lambda i, j: (i, clip(j, lo), 0, 0))
        out_specs = [zr_spec, head_spec(q0), page_spec(k0), page_spec(v0), head_spec(k0), head_spec(v0)]
        out_shape = [zr_shape,
                     jax.ShapeDtypeStruct((MOBA_H, m, MOBA_HD), F32),
                     jax.ShapeDtypeStruct((npg, MOBA_H, PAGE_SIZE, MOBA_HD), F32),
                     jax.ShapeDtypeStruct((npg, MOBA_H, PAGE_SIZE, MOBA_HD), F32),
                     jax.ShapeDtypeStruct((MOBA_H, m, MOBA_HD), BF16),
                     jax.ShapeDtypeStruct((MOBA_H, m, MOBA_HD), BF16)]
    else:
        flat_spec = lambda lo: pl.BlockSpec((tm, PROJ_TN), lambda i, j: (i, clip(j, lo)))
        out_specs = [zr_spec, flat_spec(q0), flat_spec(k0), flat_spec(v0)]
        out_shape = [zr_shape] + [jax.ShapeDtypeStruct((m, MOBA_W), F32)] * 3
    return pl.pallas_call(
        functools.partial(_proj_kernel, paged=paged, tm=tm),
        grid=(m // tm, PROJ_TILES),
        in_specs=in_specs, out_specs=out_specs, out_shape=out_shape,
        scratch_shapes=[pltpu.VMEM((tm, D_MODEL), BF16)],
        compiler_params=_cparams("parallel", "arbitrary", vmem=PROJ_VMEM_LIMIT_BYTES),
        name="proj_paged" if paged else "proj_flat",
    )(x2, g, w_rkv, w_lora, w_moba)


def _segsum(x, e):
    d = lambda u: jnp.dot(u, e, preferred_element_type=F32)
    outs = []
    for g in range(x.shape[1] // LANES):
        hi, lo = _split2(x[:, g * LANES:(g + 1) * LANES])
        outs.append(d(hi) + d(lo))
    return jnp.concatenate(outs, axis=1)


def _softplus(x):
    return jnp.maximum(x, 0.0) + jnp.log1p(jnp.exp(-jnp.abs(x)))


def _sigmoid(x):
    return 1.0 / (1.0 + jnp.exp(-x))


def _rwkv_pre(z, zprev, prm, e):
    mu, w0, a0, k_k, k_a, wl, al, gl = prm
    zs = z + mu * (zprev - z)
    r = zs[:, 0:RWKV_W]
    k = zs[:, RWKV_W:2 * RWKV_W]
    v = zs[:, 2 * RWKV_W:3 * RWKV_W]
    wd = zs[:, WD_OFF:AD_OFF]
    ad = zs[:, AD_OFF:GD_OFF]
    gd = zs[:, GD_OFF:ZR_W]
    w_log = -_softplus(-(w0 + _mm(jnp.tanh(wd), wl))) - 0.5
    lw = -jnp.exp(w_log)
    a = _sigmoid(a0 + _mm(ad, al))
    g = _mm(_sigmoid(gd), gl)
    kk = k * k_k
    kk = kk * lax.rsqrt(jnp.maximum(_segsum(kk * kk, e), 1e-24))
    k2 = k * (1.0 + (a - 1.0) * k_a)
    b = kk * a
    return r, lw, k2, v, kk, b, g


def _rwkv_post(o, r, k2, v, g, r_k, lnx_w, lnx_b, e):
    inv_hd = 1.0 / RWKV_HD
    mean = _segsum(o, e) * inv_hd
    d = o - mean
    var = _segsum(d * d, e) * inv_hd
    on = d * lax.rsqrt(var + GN_EPS) * lnx_w + lnx_b
    on = on + _segsum(r * k2 * r_k, e) * v
    return on * g


def _rwkv_prompt_kernel(z_ref, mu_ref, w0_ref, a0_ref, kk_ref, ka_ref, rk_ref, lw_ref, lb_ref,
                        wl_ref, al_ref, gl_ref, e_ref, tri_ref,
                        o_ref, s_out_ref,
                        carry_ref, s_ref, r_s, lw_s, k_s, v_s, kk_s, b_s, o_s):
    i = pl.program_id(0)
    tb = z_ref.shape[0]

    @pl.when(i == 0)
    def _():
        carry_ref[...] = jnp.zeros_like(carry_ref)
        s_ref[...] = jnp.zeros_like(s_ref)

    e = e_ref[...]
    z = z_ref[...]
    row = lax.broadcasted_iota(jnp.int32, (tb, 1), 0)
    zprev = jnp.where(row == 0, carry_ref[...], pltpu.roll(z, 1, axis=0))
    carry_ref[...] = z[tb - 1:tb, :]
    prm = (mu_ref[...], w0_ref[...], a0_ref[...], kk_ref[...], ka_ref[...],
           wl_ref[...], al_ref[...], gl_ref[...])
    r, lw, k2, v, kk, b, g = _rwkv_pre(z, zprev, prm, e)
    for p in range(N_PAIRS):
        sl = slice(p * LANES, (p + 1) * LANES)
        r_s[p] = r[:, sl]
        lw_s[p] = lw[:, sl]
        k_s[p] = k2[:, sl]
        v_s[p] = v[:, sl]
        kk_s[p] = kk[:, sl]
        b_s[p] = b[:, sl]

    lane = lax.broadcasted_iota(jnp.int32, (1, LANES), 1)
    m0 = (lane < RWKV_HD).astype(F32)
    m1 = 1.0 - m0
    ri = lax.broadcasted_iota(jnp.int32, (LANES, LANES), 0)
    ci = lax.broadcasted_iota(jnp.int32, (LANES, LANES), 1)
    rim, cim = ri & (CHUNK - 1), ci & (CHUNK - 1)
    strict = rim > cim
    incl = rim >= cim
    eye = (ri == ci).astype(F32)
    tri = tri_ref[...]

    def stack(x):
        return jnp.concatenate([x * m0, x * m1], axis=0)

    pairs = range(N_PAIRS)

    def chunk_body(c, carry):
        rows = pl.ds(pl.multiple_of(c * CHUNK, CHUNK), CHUNK)
        cum = [_mm_exact_lhs(tri, lw_s[p, rows, :]) for p in pairs]
        ar, bk, bkh, v_st, dec = [], [], [], [], []
        for p in pairs:
            lw_ = lw_s[p, rows, :]
            k_ = k_s[p, rows, :]
            b_ = b_s[p, rows, :]
            tot = cum[p][CHUNK - 1:CHUNK, :]
            gi = jnp.exp(-cum[p])
            gt = jnp.exp(tot - cum[p])
            a_st = stack(-kk_s[p, rows, :] * jnp.exp(cum[p] - lw_))
            r_st = stack(r_s[p, rows, :] * jnp.exp(cum[p]))
            ar.append(jnp.concatenate([a_st, r_st], axis=0).astype(BF16))
            bk.append(jnp.concatenate([stack(b_ * gi), stack(k_ * gi)], axis=0).astype(BF16))
            bkh.append(jnp.concatenate([stack(b_ * gt), stack(k_ * gt)], axis=0).astype(BF16))
            v_st.append(stack(v_s[p, rows, :]))
            dec.append(jnp.exp(tot))
        gs = [_mm_nt(ar[p], jnp.concatenate([bk[p], s_ref[p].astype(BF16)], axis=0)) for p in pairs]
        gm = [gs[p][:, :2 * LANES] for p in pairs]
        ah_rh = [gs[p][:, 2 * LANES:] for p in pairs]
        n = [jnp.where(strict, gm[p][:LANES, :LANES], 0.0) for p in pairs]
        akm = [jnp.where(strict, gm[p][:LANES, LANES:], 0.0) for p in pairs]
        rbk = [jnp.concatenate([jnp.where(incl, gm[p][LANES:, :LANES], 0.0),
                                jnp.where(incl, gm[p][LANES:, LANES:], 0.0)], axis=1).astype(BF16)
               for p in pairs]
        w = [_mm(akm[p], v_st[p]) for p in pairs]
        a_pw = [n[p].T for p in pairs]
        t_inv = [eye + a_pw[p] for p in pairs]
        a_pw = [_mm(a_pw[p], a_pw[p]) for p in pairs]
        for _ in range(4):
            both = [_mm(a_pw[p], jnp.concatenate([a_pw[p], t_inv[p]], axis=1)) for p in pairs]
            a_pw = [both[p][:, :LANES] for p in pairs]
            t_inv = [t_inv[p] + both[p][:, LANES:] for p in pairs]
        t_inv = [t_inv[p] + _mm(a_pw[p], t_inv[p]) for p in pairs]
        u_st = [_mm(t_inv[p].T, ah_rh[p][:LANES] + w[p]) for p in pairs]
        uv = [jnp.concatenate([u_st[p], v_st[p]], axis=0).astype(BF16) for p in pairs]
        o_st = [ah_rh[p][LANES:] + _mm(rbk[p], uv[p]) for p in pairs]
        for p in pairs:
            o_s[p, rows, :] = o_st[p][:CHUNK] + o_st[p][CHUNK:]
        uv_t = [jnp.concatenate([u_st[p].T, v_st[p].T], axis=1) for p in pairs]
        s_new = [s_ref[p] * dec[p] + _mm(uv_t[p], bkh[p]) for p in pairs]
        for p in pairs:
            s_ref[p] = s_new[p]
        return carry

    lax.fori_loop(0, tb // CHUNK, chunk_body, 0)

    o = jnp.concatenate([o_s[p] for p in range(N_PAIRS)], axis=1)
    out = _rwkv_post(o, r, k2, v, g, rk_ref[...], lw_ref[...], lb_ref[...], e)
    o_ref[...] = out.astype(o_ref.dtype)

    @pl.when(i == pl.num_programs(0) - 1)
    def _():
        s_out_ref[...] = s_ref[...]


def _full(shape):
    nd = len(shape)
    return pl.BlockSpec(shape, lambda *_: (0,) * nd)


def _rwkv_prompt(zr, rp):
    t = zr.shape[0]
    tb = RWKV_TB
    assert t % tb == 0 and tb % CHUNK == 0
    vec = lambda n: _full((1, n))
    in_specs = [pl.BlockSpec((tb, ZR_W), lambda i: (i, 0)),
                vec(ZR_W), vec(RWKV_W), vec(RWKV_W), vec(RWKV_W), vec(RWKV_W), vec(RWKV_W),
                vec(RWKV_W), vec(RWKV_W),
                _full((LANES, RWKV_W)), _full((LANES, RWKV_W)), _full((2 * LANES, RWKV_W)),
                _full((LANES, LANES)), _full((CHUNK, CHUNK))]
    pair_buf = pltpu.VMEM((N_PAIRS, tb, LANES), F32)
    return pl.pallas_call(
        _rwkv_prompt_kernel,
        grid=(t // tb,),
        in_specs=in_specs,
        out_specs=[pl.BlockSpec((tb, RWKV_W), lambda i: (i, 0)), _full((N_PAIRS, LANES, LANES))],
        out_shape=[jax.ShapeDtypeStruct((t, RWKV_W), BF16),
                   jax.ShapeDtypeStruct((N_PAIRS, LANES, LANES), F32)],
        scratch_shapes=[pltpu.VMEM((1, ZR_W), F32), pltpu.VMEM((N_PAIRS, LANES, LANES), F32)]
                       + [pair_buf] * 7,
        compiler_params=_cparams("arbitrary"),
        name="rwkv_prompt",
    )(zr, rp["mu"], rp["w0"], rp["a0"], rp["k_k"], rp["k_a"], rp["r_k"], rp["lnx_w"], rp["lnx_b"],
      rp["wl"], rp["al"], rp["gl"], rp["e"], rp["tri"])


def _rwkv_sample_pre_kernel(z_ref, prev_ref, mu_ref, w0_ref, a0_ref, kk_ref, ka_ref,
                            wl_ref, al_ref, gl_ref, e_ref,
                            r_o, w_o, k_o, v_o, kk_o, b_o, g_o):
    prm = (mu_ref[...], w0_ref[...], a0_ref[...], kk_ref[...], ka_ref[...],
           wl_ref[...], al_ref[...], gl_ref[...])
    r, lw, k2, v, kk, b, g = _rwkv_pre(z_ref[...], prev_ref[...], prm, e_ref[...])
    r_o[...] = r
    w_o[...] = jnp.exp(lw)
    k_o[...] = k2
    v_o[...] = v
    kk_o[...] = kk
    b_o[...] = b
    g_o[...] = g


def _rwkv_step_kernel(s_ref, r_ref, w_ref, k_ref, kk_ref, b_ref, v_ref, s_out, o_out):
    s = s_ref[0]
    sa = jnp.sum(s * (-kk_ref[0]), axis=-1, keepdims=True)
    s_new = s * w_ref[0] + sa * b_ref[0] + v_ref[0] * k_ref[0]
    s_out[0] = s_new
    o_out[0] = jnp.sum(s_new * r_ref[0], axis=-1, keepdims=True)


def _rwkv_sample_post_kernel(o_ref, r_ref, k_ref, v_ref, g_ref, rk_ref, lw_ref, lb_ref, e_ref, out_ref):
    out = _rwkv_post(o_ref[...], r_ref[...], k_ref[...], v_ref[...], g_ref[...],
                     rk_ref[...], lw_ref[...], lb_ref[...], e_ref[...])
    out_ref[...] = out.astype(out_ref.dtype)


def _rwkv_sample(zr, prev_pad, s0, rp):
    n = zr.shape[0]
    vec = lambda w: _full((1, w))
    rows = jax.ShapeDtypeStruct((n, RWKV_W), F32)
    r, w, k2, v, kk, b, g = pl.pallas_call(
        _rwkv_sample_pre_kernel,
        grid=(1,),
        in_specs=[_full((n, ZR_W)), _full((n, ZR_W)), vec(ZR_W), vec(RWKV_W), vec(RWKV_W), vec(RWKV_W),
                  vec(RWKV_W), _full((LANES, RWKV_W)), _full((LANES, RWKV_W)), _full((2 * LANES, RWKV_W)),
                  _full((LANES, LANES))],
        out_specs=[_full((n, RWKV_W))] * 7,
        out_shape=[rows] * 7,
        compiler_params=_cparams("arbitrary"),
        name="rwkv_sample_pre",
    )(zr, prev_pad, rp["mu"], rp["w0"], rp["a0"], rp["k_k"], rp["k_a"], rp["wl"], rp["al"], rp["gl"], rp["e"])

    as_row = lambda u: u.reshape(n, RWKV_H, 1, RWKV_HD)
    as_col = lambda u: u.reshape(n, RWKV_H, RWKV_HD, 1)
    row_spec = pl.BlockSpec((1, RWKV_H, 1, RWKV_HD), lambda i: (i, 0, 0, 0))
    col_spec = pl.BlockSpec((1, RWKV_H, RWKV_HD, 1), lambda i: (i, 0, 0, 0))
    st_spec = pl.BlockSpec((1, RWKV_H, RWKV_HD, RWKV_HD), lambda i: (i, 0, 0, 0))
    s_new, o = pl.pallas_call(
        _rwkv_step_kernel,
        grid=(n,),
        in_specs=[st_spec] + [row_spec] * 5 + [col_spec],
        out_specs=[st_spec, col_spec],
        out_shape=[jax.ShapeDtypeStruct((n, RWKV_H, RWKV_HD, RWKV_HD), F32),
                   jax.ShapeDtypeStruct((n, RWKV_H, RWKV_HD, 1), F32)],
        compiler_params=_cparams("parallel"),
        name="rwkv_sample_step",
    )(s0, as_row(r), as_row(w), as_row(k2), as_row(kk), as_row(b), as_col(v))

    out = pl.pallas_call(
        _rwkv_sample_post_kernel,
        grid=(1,),
        in_specs=[_full((n, RWKV_W))] * 5 + [vec(RWKV_W)] * 3 + [_full((LANES, LANES))],
        out_specs=_full((n, RWKV_W)),
        out_shape=jax.ShapeDtypeStruct((n, RWKV_W), BF16),
        compiler_params=_cparams("arbitrary"),
        name="rwkv_sample_post",
    )(o.reshape(n, RWKV_W), r, k2, v, g, rp["r_k"], rp["lnx_w"], rp["lnx_b"], rp["e"])
    return out, s_new


def _block_mean_kernel(k_ref, km_ref):
    x = k_ref[...]
    npg = x.shape[0]
    ppb = MOBA_BLOCK // PAGE_SIZE
    x = x.reshape(npg // ppb, MOBA_BLOCK, MOBA_HD)
    km_ref[0] = jnp.sum(x, axis=1) * (1.0 / MOBA_BLOCK)


def _block_means(k_pages):
    npg = k_pages.shape[0]
    nb = npg * PAGE_SIZE // MOBA_BLOCK
    return pl.pallas_call(
        _block_mean_kernel,
        grid=(MOBA_H,),
        in_specs=[pl.BlockSpec((npg, 1, PAGE_SIZE, MOBA_HD), lambda h: (0, h, 0, 0))],
        out_specs=pl.BlockSpec((1, nb, MOBA_HD), lambda h: (h, 0, 0)),
        out_shape=jax.ShapeDtypeStruct((MOBA_H, nb, MOBA_HD), F32),
        compiler_params=_cparams("parallel"),
        name="moba_block_means",
    )(k_pages)


MOBA_GROUP = 4
MASK_BIAS = -1e30
LOG2E = 1.4426950408889634
MOBA_TILES_PER_STEP = 2
MOBA_VMEM_LIMIT_BYTES = 56 * 1024 * 1024


def _select_bias_t(gate_t, n_valid):
    neg = jnp.float32(-jnp.inf)
    blk = lax.broadcasted_iota(jnp.int32, gate_t.shape, 0).astype(F32)
    g = jnp.where(blk < n_valid, gate_t, neg)
    bias = jnp.full(gate_t.shape, MASK_BIAS, F32)
    for _ in range(MOBA_TOPK):
        mx = jnp.max(g, axis=0, keepdims=True)
        idx = jnp.min(jnp.where(g == mx, blk, jnp.float32(1e9)), axis=0, keepdims=True)
        pick = (blk == idx) & (mx > neg)
        bias = jnp.where(pick, 0.0, bias)
        g = jnp.where(pick, neg, g)
    return bias


def _moba_prompt_kernel(q_ref, km_ref, kb_ref, vb_ref, o_ref, qa_ref, acc_ref, s_ref):
    step = pl.program_id(1)
    ng, tq, _ = q_ref.shape
    nsub = tq // MOBA_BLOCK
    first = step * nsub
    scale = MOBA_HD ** -0.5
    neg = jnp.float32(-jnp.inf)
    rr = lax.broadcasted_iota(jnp.int32, (MOBA_BLOCK, MOBA_BLOCK), 0)
    cc = lax.broadcasted_iota(jnp.int32, (MOBA_BLOCK, MOBA_BLOCK), 1)
    causal = cc <= rr
    ones_v = jnp.ones((MOBA_BLOCK, MOBA_HD), BF16)
    qcol = lax.broadcasted_iota(jnp.int32, (1, tq), 1)
    n_valid = (first + qcol // MOBA_BLOCK).astype(F32)

    heads = range(ng)
    lane_blk = lax.broadcasted_iota(jnp.int32, (MOBA_BLOCK, LANES), 1)

    def scores(g, j):
        st = pl.multiple_of(j * MOBA_BLOCK, MOBA_BLOCK)
        onehot = (lane_blk == j).astype(BF16)
        kj = jnp.concatenate([kb_ref[g, pl.ds(st, MOBA_BLOCK), :], onehot], axis=1)
        return lax.dot_general(qa_ref[g], kj, NT, preferred_element_type=F32)

    def values(g, j):
        st = pl.multiple_of(j * MOBA_BLOCK, MOBA_BLOCK)
        return jnp.concatenate([vb_ref[g, pl.ds(st, MOBA_BLOCK), :], ones_v], axis=1)

    m0 = []
    for g in heads:
        q = q_ref[g]
        bias_t = _select_bias_t(_mm3_nt(km_ref[g], q), n_valid)
        pad = jnp.full((LANES - bias_t.shape[0], tq), MASK_BIAS, F32)
        bias = jnp.concatenate([bias_t, pad], axis=0).T
        qs = (q * (scale * LOG2E)).astype(BF16)
        qa_ref[g] = jnp.concatenate([qs, bias.astype(BF16)], axis=1)
        m_sub = []
        for u in range(nsub):
            rows = slice(u * MOBA_BLOCK, (u + 1) * MOBA_BLOCK)
            st = pl.multiple_of((first + u) * MOBA_BLOCK, MOBA_BLOCK)
            kd = kb_ref[g, pl.ds(st, MOBA_BLOCK), :]
            s = jnp.where(causal, lax.dot_general(qs[rows], kd, NT, preferred_element_type=F32), neg)
            m = jnp.max(s, axis=1, keepdims=True)
            p = jnp.exp2(s - m).astype(BF16)
            m_sub.append(m)
            acc_ref[g, rows, :] = jnp.dot(p, values(g, first + u), preferred_element_type=F32)
        m0.append(jnp.concatenate(m_sub, axis=0))
    for g in heads:
        s_ref[0, g] = scores(g, 0)

    def body(j, ms):
        slot = j & 1
        s_next = [scores(g, j + 1) for g in heads]
        s = [s_ref[slot, g] for g in heads]
        m_new = [jnp.maximum(ms[g], jnp.max(s[g], axis=1, keepdims=True)) for g in heads]
        p = [jnp.exp2(s[g] - m_new[g]).astype(BF16) for g in heads]
        pv = [jnp.dot(p[g], values(g, j), preferred_element_type=F32) for g in heads]
        for g in heads:
            acc_ref[g] = jnp.exp2(ms[g] - m_new[g]) * acc_ref[g] + pv[g]
            s_ref[1 - slot, g] = s_next[g]
        return tuple(m_new)

    lax.fori_loop(0, first + nsub - 1, body, tuple(m0))
    for g in heads:
        acc = acc_ref[g]
        o_ref[:, g * MOBA_HD:(g + 1) * MOBA_HD] = (acc[:, :MOBA_HD] / acc[:, MOBA_HD:]).astype(o_ref.dtype)


def _moba_prompt(q3, km, kb, vb):
    t = q3.shape[1]
    nb = t // MOBA_BLOCK
    tq = MOBA_TILES_PER_STEP * MOBA_BLOCK
    assert t % tq == 0 and nb <= LANES and nb % 8 == 0 and MOBA_H % MOBA_GROUP == 0
    ng = MOBA_GROUP
    resident = lambda: pl.BlockSpec((ng, t, MOBA_HD), lambda h, i: (h, 0, 0), pipeline_mode=pl.Buffered(1))
    return pl.pallas_call(
        _moba_prompt_kernel,
        grid=(MOBA_H // ng, t // tq),
        in_specs=[pl.BlockSpec((ng, tq, MOBA_HD), lambda h, i: (h, i, 0)),
                  pl.BlockSpec((ng, nb, MOBA_HD), lambda h, i: (h, 0, 0)),
                  resident(), resident()],
        out_specs=pl.BlockSpec((tq, ng * MOBA_HD), lambda h, i: (i, h)),
        out_shape=jax.ShapeDtypeStruct((t, MOBA_W), BF16),
        scratch_shapes=[pltpu.VMEM((ng, tq, 2 * MOBA_HD), BF16),
                        pltpu.VMEM((ng, tq, 2 * MOBA_HD), F32),
                        pltpu.VMEM((2, ng, tq, MOBA_BLOCK), F32)],
        compiler_params=_cparams("parallel", "arbitrary", vmem=MOBA_VMEM_LIMIT_BYTES),
        name="moba_prompt",
    )(q3, km, kb, vb)


PAGES_PER_STEP = 8


def _page_sum_kernel(pt_ref, *refs):
    del pt_ref
    k_refs, out_ref = refs[:PAGES_PER_STEP], refs[PAGES_PER_STEP]
    for h in range(MOBA_H):
        rows = [jnp.sum(k_refs[u][0, 0, h], axis=0, keepdims=True) for u in range(PAGES_PER_STEP)]
        out_ref[0, h] = jnp.concatenate(rows, axis=0)


def _page_sums(cache_k, page_table):
    nbt, n_pages = page_table.shape
    assert n_pages % PAGES_PER_STEP == 0
    spec = lambda u: pl.BlockSpec((1, 1, MOBA_H, PAGE_SIZE, MOBA_HD),
                                  lambda b, g, pt: (pt[b, g * PAGES_PER_STEP + u], 0, 0, 0, 0))
    return pl.pallas_call(
        _page_sum_kernel,
        grid_spec=pltpu.PrefetchScalarGridSpec(
            num_scalar_prefetch=1,
            grid=(nbt, n_pages // PAGES_PER_STEP),
            in_specs=[spec(u) for u in range(PAGES_PER_STEP)],
            out_specs=pl.BlockSpec((1, MOBA_H, PAGES_PER_STEP, MOBA_HD), lambda b, g, pt: (b, 0, g, 0))),
        out_shape=jax.ShapeDtypeStruct((nbt, MOBA_H, n_pages, MOBA_HD), F32),
        compiler_params=_cparams("parallel", "arbitrary"),
        name="moba_page_sums",
    )(page_table, *([cache_k] * PAGES_PER_STEP))


def _sample_gate_kernel(q_ref, ps_ref, sel_ref):
    n_pages = ps_ref.shape[2]
    lane = lax.broadcasted_iota(jnp.int32, (1, n_pages), 1)
    even = (lane & 1) == 0
    neg = jnp.float32(-jnp.inf)
    big = jnp.int32(2 ** 30)
    out_lane = lax.broadcasted_iota(jnp.int32, (1, LANES), 1)
    rows = []
    for h in range(MOBA_H):
        qh = jnp.broadcast_to(q_ref[0, h:h + 1, :], (8, MOBA_HD))
        gp = _mm3_nt(qh, ps_ref[0, h])[0:1, :]
        nbr = jnp.where(even, pltpu.roll(gp, n_pages - 1, axis=1), pltpu.roll(gp, 1, axis=1))
        g = jnp.where(even, (gp + nbr) * (1.0 / MOBA_BLOCK), neg)
        res = jnp.zeros((1, LANES), jnp.int32)
        for s in range(MOBA_TOPK):
            mx = jnp.max(g, axis=1, keepdims=True)
            idx = jnp.min(jnp.where(g == mx, lane, big), axis=1, keepdims=True)
            g = jnp.where(lane == idx, neg, g)
            res = jnp.where(out_lane == s, idx >> 1, res)
        rows.append(res)
    sel_ref[0] = jnp.concatenate(rows, axis=0)


def _sample_gate(q3, psums):
    nbt, _, n_pages, _ = psums.shape
    assert n_pages == LANES and n_pages // 2 >= MOBA_TOPK
    return pl.pallas_call(
        _sample_gate_kernel,
        grid=(nbt,),
        in_specs=[pl.BlockSpec((1, MOBA_H, MOBA_HD), lambda b: (b, 0, 0)),
                  pl.BlockSpec((1, MOBA_H, n_pages, MOBA_HD), lambda b: (b, 0, 0, 0))],
        out_specs=pl.BlockSpec((1, MOBA_H, LANES), lambda b: (b, 0, 0)),
        out_shape=jax.ShapeDtypeStruct((nbt, MOBA_H, LANES), jnp.int32),
        compiler_params=_cparams("parallel"),
        name="moba_sample_gate",
    )(q3, psums)


SAMPLE_PAGES = MOBA_TOPK * (MOBA_BLOCK // PAGE_SIZE)


def _sample_attn_kernel(ph_ref, q_ref, kn_ref, vn_ref, *refs):
    del ph_ref
    n_in = MOBA_H * SAMPLE_PAGES
    k_refs, v_refs, o_ref = refs[:n_in], refs[n_in:2 * n_in], refs[2 * n_in]
    scale = MOBA_HD ** -0.5
    for h in range(MOBA_H):
        q = q_ref[0, h]
        kp = k_refs[h * SAMPLE_PAGES:(h + 1) * SAMPLE_PAGES]
        vp = v_refs[h * SAMPLE_PAGES:(h + 1) * SAMPLE_PAGES]
        s_self = jnp.sum(kn_ref[0, h] * q, axis=1, keepdims=True) * scale
        ss = [jnp.sum(kp[u][0, 0, 0] * q, axis=1, keepdims=True) * scale for u in range(SAMPLE_PAGES)]
        m = s_self
        for s in ss:
            m = jnp.maximum(m, jnp.max(s, axis=0, keepdims=True))
        p_self = jnp.exp(s_self - m)
        l = p_self
        acc = p_self * vn_ref[0, h]
        for u in range(SAMPLE_PAGES):
            p = jnp.exp(ss[u] - m)
            l = l + jnp.sum(p, axis=0, keepdims=True)
            acc = acc + jnp.sum(p * vp[u][0, 0, 0], axis=0, keepdims=True)
        o_ref[0, h] = acc / l


def _sample_attn(q4, kn4, vn4, cache_k, cache_v, phys):
    nbt = q4.shape[0]
    vec_spec = pl.BlockSpec((1, MOBA_H, 1, MOBA_HD), lambda b, ph: (b, 0, 0, 0))
    page_spec = lambda h, u: pl.BlockSpec(
        (1, 1, 1, PAGE_SIZE, MOBA_HD),
        lambda b, ph: (ph[(b * MOBA_H + h) * SAMPLE_PAGES + u], 0, h, 0, 0))
    pages = [page_spec(h, u) for h in range(MOBA_H) for u in range(SAMPLE_PAGES)]
    return pl.pallas_call(
        _sample_attn_kernel,
        grid_spec=pltpu.PrefetchScalarGridSpec(
            num_scalar_prefetch=1,
            grid=(nbt,),
            in_specs=[vec_spec, vec_spec, vec_spec] + pages + pages,
            out_specs=vec_spec),
        out_shape=jax.ShapeDtypeStruct((nbt, MOBA_H, 1, MOBA_HD), F32),
        compiler_params=_cparams("arbitrary"),
        name="moba_sample_attn",
    )(phys, q4, kn4, vn4, *([cache_k] * len(pages)), *([cache_v] * len(pages)))


def _outproj_kernel(x_ref, or_ref, om_ref, wr_ref, wm_ref, g_ref, o_ref):
    y = (jnp.dot(or_ref[...], wr_ref[...], preferred_element_type=F32)
         + jnp.dot(om_ref[...], wm_ref[...], preferred_element_type=F32))
    o_ref[...] = x_ref[...] + _rms(y, g_ref[...])


def _outproj(x2, o_r, o_m, w_r, w_m, g):
    m = x2.shape[0]
    tm = min(m, 512)
    assert m % tm == 0
    return pl.pallas_call(
        _outproj_kernel,
        grid=(m // tm,),
        in_specs=[pl.BlockSpec((tm, D_MODEL), lambda i: (i, 0)),
                  pl.BlockSpec((tm, RWKV_W), lambda i: (i, 0)),
                  pl.BlockSpec((tm, MOBA_W), lambda i: (i, 0)),
                  _full((RWKV_W, D_MODEL)), _full((MOBA_W, D_MODEL)), _full((1, D_MODEL))],
        out_specs=pl.BlockSpec((tm, D_MODEL), lambda i: (i, 0)),
        out_shape=jax.ShapeDtypeStruct((m, D_MODEL), F32),
        compiler_params=_cparams("parallel"),
        name="outproj",
    )(x2, o_r, o_m, w_r, w_m, g)


MLP_TF = 1024
MLP_TM = 512


def _mlp_kernel(x_ref, g3_ref, g4_ref, wu_ref, wd_ref, o_ref, xn_ref, acc_ref):
    j = pl.program_id(1)

    @pl.when(j == 0)
    def _():
        xn_ref[...] = _rms(x_ref[...], g3_ref[...]).astype(BF16)
        acc_ref[...] = jnp.zeros_like(acc_ref)

    h = jnp.dot(xn_ref[...], wu_ref[...], preferred_element_type=F32)
    h = jnp.square(jnp.maximum(h, 0.0))
    acc_ref[...] += jnp.dot(h.astype(BF16), wd_ref[...], preferred_element_type=F32)

    @pl.when(j == pl.num_programs(1) - 1)
    def _():
        o_ref[...] = x_ref[...] + _rms(acc_ref[...], g4_ref[...])


def _mlp(x1, g3, g4, w_up, w_down):
    m = x1.shape[0]
    tm = min(m, MLP_TM)
    assert m % tm == 0
    return pl.pallas_call(
        _mlp_kernel,
        grid=(m // tm, D_FF // MLP_TF),
        in_specs=[pl.BlockSpec((tm, D_MODEL), lambda i, j: (i, 0)),
                  pl.BlockSpec((1, D_MODEL), lambda i, j: (0, 0)),
                  pl.BlockSpec((1, D_MODEL), lambda i, j: (0, 0)),
                  pl.BlockSpec((D_MODEL, MLP_TF), lambda i, j: (0, j)),
                  pl.BlockSpec((MLP_TF, D_MODEL), lambda i, j: (j, 0))],
        out_specs=pl.BlockSpec((tm, D_MODEL), lambda i, j: (i, 0)),
        out_shape=jax.ShapeDtypeStruct((m, D_MODEL), F32),
        scratch_shapes=[pltpu.VMEM((tm, D_MODEL), BF16), pltpu.VMEM((tm, D_MODEL), F32)],
        compiler_params=_cparams("parallel", "arbitrary"),
        name="mlp",
    )(x1, g3, g4, w_up, w_down)


def _pad_cols(a, lora_only=False):
    z = lambda n: jnp.zeros(a.shape[:-1] + (n,), a.dtype)
    lo = 0 if lora_only else 3 * RWKV_W
    wd = a[..., lo:lo + DECAY_LORA]
    ad = a[..., lo + DECAY_LORA:lo + DECAY_LORA + AAA_LORA]
    gd = a[..., lo + DECAY_LORA + AAA_LORA:lo + DECAY_LORA + AAA_LORA + GATE_LORA]
    parts = [wd, z(LANES - DECAY_LORA), ad, z(LANES - AAA_LORA), gd, z(2 * LANES - GATE_LORA)]
    if not lora_only:
        parts.insert(0, a[..., :lo])
    return jnp.concatenate(parts, axis=-1)


def _unpad_cols(zr):
    return jnp.concatenate([zr[..., :3 * RWKV_W + DECAY_LORA], zr[..., AD_OFF:AD_OFF + AAA_LORA],
                            zr[..., GD_OFF:GD_OFF + GATE_LORA]], axis=-1)


def _pad_rows(a, n):
    return jnp.concatenate([a, jnp.zeros((n - a.shape[0],) + a.shape[1:], a.dtype)], axis=0)


def _unpair_state(s_bd):
    h0 = s_bd[:, :RWKV_HD, :RWKV_HD]
    h1 = s_bd[:, RWKV_HD:, RWKV_HD:]
    return jnp.stack([h0, h1], axis=1).reshape(RWKV_H, RWKV_HD, RWKV_HD)


def kernel(x_prompt, x_sample, cache_k, cache_v, page_table, state_wkv, state_shift, g_mix_pre, g_mix_post, g_ffn_pre, g_ffn_post, w_in, mu_shift, w0, w_lora2, a0, a_lora2, g_lora2, k_k, k_a, r_k, lnx_w, lnx_b, w_out, w_up, w_down):
    depth = w_in.shape[0]
    bsz, seq, _ = x_prompt.shape
    dbt, dseq, _ = x_sample.shape
    assert depth == 1 and bsz == 1 and dseq == 1, "kernels are written for the stated shapes"
    n_pages = page_table.shape[1]
    l = 0

    row = lambda a: a.reshape(1, -1)
    w_rkv = w_in[l, :, :3 * RWKV_W].astype(BF16)
    w_lora = _pad_cols(w_in[l, :, 3 * RWKV_W:SHIFT_W].astype(BF16), lora_only=True)
    w_moba = w_in[l, :, SHIFT_W:].astype(BF16)
    lane = jnp.arange(LANES)
    e_seg = ((lane[:, None] // RWKV_HD) == (lane[None, :] // RWKV_HD)).astype(BF16)
    ci = jnp.arange(CHUNK)
    tri = (ci[:, None] >= ci[None, :]).astype(BF16)
    rp = dict(mu=row(_pad_cols(mu_shift[l])), w0=row(w0[l]), a0=row(a0[l]), k_k=row(k_k[l]),
              k_a=row(k_a[l]), r_k=row(r_k[l]), lnx_w=row(lnx_w[l]), lnx_b=row(lnx_b[l]),
              wl=_pad_rows(w_lora2[l], LANES).astype(BF16), al=_pad_rows(a_lora2[l], LANES).astype(BF16),
              gl=_pad_rows(g_lora2[l], 2 * LANES).astype(BF16), e=e_seg, tri=tri)
    w_or = w_out[l, :RWKV_W].astype(BF16)
    w_om = w_out[l, RWKV_W:].astype(BF16)
    w_up_b = w_up[l].astype(BF16)
    w_down_b = w_down[l].astype(BF16)
    g1, g2, g3, g4 = row(g_mix_pre[l]), row(g_mix_post[l]), row(g_ffn_pre[l]), row(g_ffn_post[l])

    xp = x_prompt.reshape(seq, D_MODEL)
    zr, q3, k_pages, v_pages, kb, vb = _proj(xp, g1, w_rkv, w_lora, w_moba, paged=True)
    o_r, s_bd = _rwkv_prompt(zr, rp)
    km = _block_means(k_pages)
    o_m = _moba_prompt(q3, km, kb, vb)
    x1 = _outproj(xp, o_r, o_m, w_or, w_om, g2)
    y_prompt = _mlp(x1, g3, g4, w_up_b, w_down_b).reshape(bsz, seq, D_MODEL)
    npg = seq // PAGE_SIZE
    k_prompt = k_pages.reshape(bsz, npg, 1, MOBA_H, PAGE_SIZE, MOBA_HD)
    v_prompt = v_pages.reshape(bsz, npg, 1, MOBA_H, PAGE_SIZE, MOBA_HD)
    wkv_prompt = _unpair_state(s_bd).reshape(1, bsz, RWKV_H, RWKV_HD, RWKV_HD)
    shift_prompt = _unpad_cols(zr[seq - 1:seq]).reshape(1, bsz, SHIFT_W)

    xs = x_sample.reshape(dbt, D_MODEL)
    zr_s, q_s, k_s, v_s = _proj(xs, g1, w_rkv, w_lora, w_moba, paged=False)
    o_rs, wkv_s = _rwkv_sample(zr_s, _pad_cols(state_shift[l]), state_wkv[l], rp)
    psums = _page_sums(cache_k, page_table)
    sel = _sample_gate(q_s.reshape(dbt, MOBA_H, MOBA_HD), psums)[:, :, :MOBA_TOPK]
    ppb = MOBA_BLOCK // PAGE_SIZE
    logical = sel[..., None] * ppb + jnp.arange(ppb, dtype=jnp.int32)
    phys = jnp.take_along_axis(page_table, logical.reshape(dbt, -1), axis=1).reshape(-1)
    as4 = lambda u: u.reshape(dbt, MOBA_H, 1, MOBA_HD)
    o_ms = _sample_attn(as4(q_s), as4(k_s), as4(v_s), cache_k, cache_v, phys)
    o_ms = o_ms.reshape(dbt, MOBA_W).astype(BF16)
    x1s = _outproj(xs, o_rs, o_ms, w_or, w_om, g2)
    y_sample = _mlp(x1s, g3, g4, w_up_b, w_down_b).reshape(dbt, dseq, D_MODEL)
    k_sample = k_s.reshape(dbt, 1, MOBA_H, dseq, MOBA_HD)
    v_sample = v_s.reshape(dbt, 1, MOBA_H, dseq, MOBA_HD)
    wkv_sample = wkv_s.reshape(1, dbt, RWKV_H, RWKV_HD, RWKV_HD)
    shift_sample = _unpad_cols(zr_s).reshape(1, dbt, SHIFT_W)

    return (y_prompt, y_sample, k_prompt, v_prompt, wkv_prompt, shift_prompt,
            k_sample, v_sample, wkv_sample, shift_sample)
```

```python
import functools

import jax
import jax.numpy as jnp
from jax import lax
from jax.experimental import pallas as pl
from jax.experimental.pallas import tpu as pltpu

F32 = jnp.float32
BF16 = jnp.bfloat16

D_MODEL = 2048
RWKV_W = 1024
RWKV_HD = 64
RWKV_H = RWKV_W // RWKV_HD
DECAY_LORA = 64
AAA_LORA = 64
GATE_LORA = 160
MOBA_W = 1024
MOBA_HD = 128
MOBA_H = MOBA_W // MOBA_HD
MOBA_BLOCK = 256
MOBA_TOPK = 3
PAGE_SIZE = 128
D_FF = 4 * D_MODEL
RMS_EPS = 1e-6
GN_EPS = 64e-5
SHIFT_W = 3 * RWKV_W + DECAY_LORA + AAA_LORA + GATE_LORA

LANES = 128
VMEM_LIMIT_BYTES = 48 * 1024 * 1024

WD_OFF = 3 * RWKV_W
AD_OFF = WD_OFF + LANES
GD_OFF = AD_OFF + LANES
ZR_W = GD_OFF + 2 * LANES
PROJ_TN = 512
ZR_TILES = ZR_W // PROJ_TN
RKV_TILES = 3 * RWKV_W // PROJ_TN
MOBA_TILES = MOBA_W // PROJ_TN
PROJ_PAD_W = ZR_W + 3 * MOBA_W
PROJ_TILES = PROJ_PAD_W // PROJ_TN
PROJ_VMEM_LIMIT_BYTES = 56 * 1024 * 1024

CHUNK = 64
RWKV_TB = 256
N_PAIRS = RWKV_W // LANES

NT = (((1,), (1,)), ((), ()))


def _cparams(*sem, vmem=VMEM_LIMIT_BYTES):
    return pltpu.CompilerParams(dimension_semantics=sem, vmem_limit_bytes=vmem)


def _mm(a, b):
    return jnp.dot(a.astype(BF16), b.astype(BF16), preferred_element_type=F32)


def _mm_nt(a, b):
    return lax.dot_general(a.astype(BF16), b.astype(BF16), NT, preferred_element_type=F32)


def _split2(x):
    hi = x.astype(BF16)
    lo = (x - hi.astype(F32)).astype(BF16)
    return hi, lo


def _mm3_nt(a, b):
    ah, al = _split2(a)
    bh, bl = _split2(b)
    d = lambda u, v: lax.dot_general(u, v, NT, preferred_element_type=F32)
    return d(ah, bh) + (d(ah, bl) + d(al, bh))


def _mm_exact_lhs(e, x):
    h0 = x.astype(BF16)
    r0 = x - h0.astype(F32)
    h1 = r0.astype(BF16)
    h2 = (r0 - h1.astype(F32)).astype(BF16)
    d = lambda u: jnp.dot(e, u, preferred_element_type=F32)
    return d(h0) + (d(h1) + d(h2))


def _rms(x, g):
    ms = jnp.mean(x * x, axis=-1, keepdims=True)
    return (x * lax.rsqrt(ms + RMS_EPS)) * g


def _proj_kernel(x_ref, g_ref, wa_ref, wb_ref, wc_ref, *refs, paged, tm):
    if paged:
        zr_ref, q_ref, kp_ref, vp_ref, kb_ref, vb_ref, xn_ref = refs
    else:
        zr_ref, q_ref, k_ref, v_ref, xn_ref = refs
    j = pl.program_id(1)

    @pl.when(j == 0)
    def _():
        xn_ref[...] = _rms(x_ref[...], g_ref[...]).astype(BF16)

    tile = lambda w_ref: jnp.dot(xn_ref[...], w_ref[...], preferred_element_type=F32)
    heads_per_tile = PROJ_TN // MOBA_HD
    q0, k0, v0 = ZR_TILES, ZR_TILES + MOBA_TILES, ZR_TILES + 2 * MOBA_TILES

    @pl.when(j < RKV_TILES)
    def _():
        zr_ref[...] = tile(wa_ref)

    @pl.when(j == RKV_TILES)
    def _():
        zr_ref[...] = tile(wb_ref)

    if not paged:
        @pl.when((j >= q0) & (j < k0))
        def _():
            q_ref[...] = tile(wc_ref)

        @pl.when((j >= k0) & (j < v0))
        def _():
            k_ref[...] = tile(wc_ref)

        @pl.when(j >= v0)
        def _():
            v_ref[...] = tile(wc_ref)
        return

    @pl.when((j >= q0) & (j < k0))
    def _():
        acc = tile(wc_ref)
        for hh in range(heads_per_tile):
            q_ref[hh] = acc[:, hh * MOBA_HD:(hh + 1) * MOBA_HD]

    def store_kv(page_ref, bf_ref):
        acc = tile(wc_ref)
        for hh in range(heads_per_tile):
            blk = acc[:, hh * MOBA_HD:(hh + 1) * MOBA_HD]
            bf_ref[hh] = blk.astype(BF16)
            for pg in range(tm // PAGE_SIZE):
                page_ref[pg, hh] = blk[pg * PAGE_SIZE:(pg + 1) * PAGE_SIZE]

    @pl.when((j >= k0) & (j < v0))
    def _():
        store_kv(kp_ref, kb_ref)

    @pl.when(j >= v0)
    def _():
        store_kv(vp_ref, vb_ref)


def _proj(x2, g, w_rkv, w_lora, w_moba, *, paged):
    m = x2.shape[0]
    tm = 1024 if paged else m
    assert m % tm == 0
    q0, k0, v0 = ZR_TILES, ZR_TILES + MOBA_TILES, ZR_TILES + 2 * MOBA_TILES
    hpt = PROJ_TN // MOBA_HD
    clip = lambda j, lo: jnp.clip(j - lo, 0, MOBA_TILES - 1)
    in_specs = [
        pl.BlockSpec((tm, D_MODEL), lambda i, j: (i, 0)),
        pl.BlockSpec((1, D_MODEL), lambda i, j: (0, 0)),
        pl.BlockSpec((D_MODEL, PROJ_TN), lambda i, j: (0, jnp.minimum(j, RKV_TILES - 1))),
        pl.BlockSpec((D_MODEL, PROJ_TN), lambda i, j: (0, 0)),
        pl.BlockSpec((D_MODEL, PROJ_TN), lambda i, j: (0, jnp.clip(j - q0, 0, 3 * MOBA_TILES - 1))),
    ]
    zr_spec = pl.BlockSpec((tm, PROJ_TN), lambda i, j: (i, jnp.minimum(j, ZR_TILES - 1)))
    zr_shape = jax.ShapeDtypeStruct((m, ZR_W), F32)
    if paged:
        npg = m // PAGE_SIZE
        head_spec = lambda lo: pl.BlockSpec((hpt, tm, MOBA_HD), lambda i, j: (clip(j, lo), i, 0))
        page_spec = lambda lo: pl.BlockSpec((tm // PAGE_SIZE, hpt, PAGE_SIZE, MOBA_HD),
                                            lambda i, j: (i, clip(j, lo), 0, 0))
        out_specs = [zr_spec, head_spec(q0), page_spec(k0), page_spec(v0), head_spec(k0), head_spec(v0)]
        out_shape = [zr_shape,
                     jax.ShapeDtypeStruct((MOBA_H, m, MOBA_HD), F32),
                     jax.ShapeDtypeStruct((npg, MOBA_H, PAGE_SIZE, MOBA_HD), F32),
                     jax.ShapeDtypeStruct((npg, MOBA_H, PAGE_SIZE, MOBA_HD), F32),
                     jax.ShapeDtypeStruct((MOBA_H, m, MOBA_HD), BF16),
                     jax.ShapeDtypeStruct((MOBA_H, m, MOBA_HD), BF16)]
    else:
        flat_spec = lambda lo: pl.BlockSpec((tm, PROJ_TN), lambda i, j: (i, clip(j, lo)))
        out_specs = [zr_spec, flat_spec(q0), flat_spec(k0), flat_spec(v0)]
        out_shape = [zr_shape] + [jax.ShapeDtypeStruct((m, MOBA_W), F32)] * 3
    return pl.pallas_call(
        functools.partial(_proj_kernel, paged=paged, tm=tm),
        grid=(m // tm, PROJ_TILES),
        in_specs=in_specs, out_specs=out_specs, out_shape=out_shape,
        scratch_shapes=[pltpu.VMEM((tm, D_MODEL), BF16)],
        compiler_params=_cparams("parallel", "arbitrary", vmem=PROJ_VMEM_LIMIT_BYTES),
        name="proj_paged" if paged else "proj_flat",
    )(x2, g, w_rkv, w_lora, w_moba)


def _segsum(x, e):
    d = lambda u: jnp.dot(u, e, preferred_element_type=F32)
    outs = []
    for g in range(x.shape[1] // LANES):
        hi, lo = _split2(x[:, g * LANES:(g + 1) * LANES])
        outs.append(d(hi) + d(lo))
    return jnp.concatenate(outs, axis=1)


def _softplus(x):
    return jnp.maximum(x, 0.0) + jnp.log1p(jnp.exp(-jnp.abs(x)))


def _sigmoid(x):
    return 1.0 / (1.0 + jnp.exp(-x))


def _rwkv_pre(z, zprev, prm, e):
    mu, w0, a0, k_k, k_a, wl, al, gl = prm
    zs = z + mu * (zprev - z)
    r = zs[:, 0:RWKV_W]
    k = zs[:, RWKV_W:2 * RWKV_W]
    v = zs[:, 2 * RWKV_W:3 * RWKV_W]
    wd = zs[:, WD_OFF:AD_OFF]
    ad = zs[:, AD_OFF:GD_OFF]
    gd = zs[:, GD_OFF:ZR_W]
    w_log = -_softplus(-(w0 + _mm(jnp.tanh(wd), wl))) - 0.5
    lw = -jnp.exp(w_log)
    a = _sigmoid(a0 + _mm(ad, al))
    g = _mm(_sigmoid(gd), gl)
    kk = k * k_k
    kk = kk * lax.rsqrt(jnp.maximum(_segsum(kk * kk, e), 1e-24))
    k2 = k * (1.0 + (a - 1.0) * k_a)
    b = kk * a
    return r, lw, k2, v, kk, b, g


def _rwkv_post(o, r, k2, v, g, r_k, lnx_w, lnx_b, e):
    inv_hd = 1.0 / RWKV_HD
    mean = _segsum(o, e) * inv_hd
    d = o - mean
    var = _segsum(d * d, e) * inv_hd
    on = d * lax.rsqrt(var + GN_EPS) * lnx_w + lnx_b
    on = on + _segsum(r * k2 * r_k, e) * v
    return on * g


def _rwkv_prompt_kernel(z_ref, mu_ref, w0_ref, a0_ref, kk_ref, ka_ref, rk_ref, lw_ref, lb_ref,
                        wl_ref, al_ref, gl_ref, e_ref, tri_ref,
                        o_ref, s_out_ref,
                        carry_ref, s_ref, r_s, lw_s, k_s, v_s, kk_s, b_s, o_s):
    i = pl.program_id(0)
    tb = z_ref.shape[0]

    @pl.when(i == 0)
    def _():
        carry_ref[...] = jnp.zeros_like(carry_ref)
        s_ref[...] = jnp.zeros_like(s_ref)

    e = e_ref[...]
    z = z_ref[...]
    row = lax.broadcasted_iota(jnp.int32, (tb, 1), 0)
    zprev = jnp.where(row == 0, carry_ref[...], pltpu.roll(z, 1, axis=0))
    carry_ref[...] = z[tb - 1:tb, :]
    prm = (mu_ref[...], w0_ref[...], a0_ref[...], kk_ref[...], ka_ref[...],
           wl_ref[...], al_ref[...], gl_ref[...])
    r, lw, k2, v, kk, b, g = _rwkv_pre(z, zprev, prm, e)
    for p in range(N_PAIRS):
        sl = slice(p * LANES, (p + 1) * LANES)
        r_s[p] = r[:, sl]
        lw_s[p] = lw[:, sl]
        k_s[p] = k2[:, sl]
        v_s[p] = v[:, sl]
        kk_s[p] = kk[:, sl]
        b_s[p] = b[:, sl]

    lane = lax.broadcasted_iota(jnp.int32, (1, LANES), 1)
    m0 = (lane < RWKV_HD).astype(F32)
    m1 = 1.0 - m0
    ri = lax.broadcasted_iota(jnp.int32, (LANES, LANES), 0)
    ci = lax.broadcasted_iota(jnp.int32, (LANES, LANES), 1)
    rim, cim = ri & (CHUNK - 1), ci & (CHUNK - 1)
    strict = rim > cim
    incl = rim >= cim
    eye = (ri == ci).astype(F32)
    tri = tri_ref[...]

    def stack(x):
        return jnp.concatenate([x * m0, x * m1], axis=0)

    pairs = range(N_PAIRS)

    def chunk_body(c, carry):
        rows = pl.ds(pl.multiple_of(c * CHUNK, CHUNK), CHUNK)
        cum = [_mm_exact_lhs(tri, lw_s[p, rows, :]) for p in pairs]
        ar, bk, bkh, v_st, dec = [], [], [], [], []
        for p in pairs:
            lw_ = lw_s[p, rows, :]
            k_ = k_s[p, rows, :]
            b_ = b_s[p, rows, :]
            tot = cum[p][CHUNK - 1:CHUNK, :]
            gi = jnp.exp(-cum[p])
            gt = jnp.exp(tot - cum[p])
            a_st = stack(-kk_s[p, rows, :] * jnp.exp(cum[p] - lw_))
            r_st = stack(r_s[p, rows, :] * jnp.exp(cum[p]))
            ar.append(jnp.concatenate([a_st, r_st], axis=0).astype(BF16))
            bk.append(jnp.concatenate([stack(b_ * gi), stack(k_ * gi)], axis=0).astype(BF16))
            bkh.append(jnp.concatenate([stack(b_ * gt), stack(k_ * gt)], axis=0).astype(BF16))
            v_st.append(stack(v_s[p, rows, :]))
            dec.append(jnp.exp(tot))
        gs = [_mm_nt(ar[p], jnp.concatenate([bk[p], s_ref[p].astype(BF16)], axis=0)) for p in pairs]
        gm = [gs[p][:, :2 * LANES] for p in pairs]
        ah_rh = [gs[p][:, 2 * LANES:] for p in pairs]
        n = [jnp.where(strict, gm[p][:LANES, :LANES], 0.0) for p in pairs]
        akm = [jnp.where(strict, gm[p][:LANES, LANES:], 0.0) for p in pairs]
        rbk = [jnp.concatenate([jnp.where(incl, gm[p][LANES:, :LANES], 0.0),
                                jnp.where(incl, gm[p][LANES:, LANES:], 0.0)], axis=1).astype(BF16)
               for p in pairs]
        w = [_mm(akm[p], v_st[p]) for p in pairs]
        a_pw = [n[p].T for p in pairs]
        t_inv = [eye + a_pw[p] for p in pairs]
        a_pw = [_mm(a_pw[p], a_pw[p]) for p in pairs]
        for _ in range(4):
            both = [_mm(a_pw[p], jnp.concatenate([a_pw[p], t_inv[p]], axis=1)) for p in pairs]
            a_pw = [both[p][:, :LANES] for p in pairs]
            t_inv = [t_inv[p] + both[p][:, LANES:] for p in pairs]
        t_inv = [t_inv[p] + _mm(a_pw[p], t_inv[p]) for p in pairs]
        u_st = [_mm(t_inv[p].T, ah_rh[p][:LANES] + w[p]) for p in pairs]
        uv = [jnp.concatenate([u_st[p], v_st[p]], axis=0).astype(BF16) for p in pairs]
        o_st = [ah_rh[p][LANES:] + _mm(rbk[p], uv[p]) for p in pairs]
        for p in pairs:
            o_s[p, rows, :] = o_st[p][:CHUNK] + o_st[p][CHUNK:]
        uv_t = [jnp.concatenate([u_st[p].T, v_st[p].T], axis=1) for p in pairs]
        s_new = [s_ref[p] * dec[p] + _mm(uv_t[p], bkh[p]) for p in pairs]
        for p in pairs:
            s_ref[p] = s_new[p]
        return carry

    lax.fori_loop(0, tb // CHUNK, chunk_body, 0)

    o = jnp.concatenate([o_s[p] for p in range(N_PAIRS)], axis=1)
    out = _rwkv_post(o, r, k2, v, g, rk_ref[...], lw_ref[...], lb_ref[...], e)
    o_ref[...] = out.astype(o_ref.dtype)

    @pl.when(i == pl.num_programs(0) - 1)
    def _():
        s_out_ref[...] = s_ref[...]


def _full(shape):
    nd = len(shape)
    return pl.BlockSpec(shape, lambda *_: (0,) * nd)


def _rwkv_prompt(zr, rp):
    t = zr.shape[0]
    tb = RWKV_TB
    assert t % tb == 0 and tb % CHUNK == 0
    vec = lambda n: _full((1, n))
    in_specs = [pl.BlockSpec((tb, ZR_W), lambda i: (i, 0)),
                vec(ZR_W), vec(RWKV_W), vec(RWKV_W), vec(RWKV_W), vec(RWKV_W), vec(RWKV_W),
                vec(RWKV_W), vec(RWKV_W),
                _full((LANES, RWKV_W)), _full((LANES, RWKV_W)), _full((2 * LANES, RWKV_W)),
                _full((LANES, LANES)), _full((CHUNK, CHUNK))]
    pair_buf = pltpu.VMEM((N_PAIRS, tb, LANES), F32)
    return pl.pallas_call(
        _rwkv_prompt_kernel,
        grid=(t // tb,),
        in_specs=in_specs,
        out_specs=[pl.BlockSpec((tb, RWKV_W), lambda i: (i, 0)), _full((N_PAIRS, LANES, LANES))],
        out_shape=[jax.ShapeDtypeStruct((t, RWKV_W), BF16),
                   jax.ShapeDtypeStruct((N_PAIRS, LANES, LANES), F32)],
        scratch_shapes=[pltpu.VMEM((1, ZR_W), F32), pltpu.VMEM((N_PAIRS, LANES, LANES), F32)]
                       + [pair_buf] * 7,
        compiler_params=_cparams("arbitrary"),
        name="rwkv_prompt",
    )(zr, rp["mu"], rp["w0"], rp["a0"], rp["k_k"], rp["k_a"], rp["r_k"], rp["lnx_w"], rp["lnx_b"],
      rp["wl"], rp["al"], rp["gl"], rp["e"], rp["tri"])


def _rwkv_sample_pre_kernel(z_ref, prev_ref, mu_ref, w0_ref, a0_ref, kk_ref, ka_ref,
                            wl_ref, al_ref, gl_ref, e_ref,
                            r_o, w_o, k_o, v_o, kk_o, b_o, g_o):
    prm = (mu_ref[...], w0_ref[...], a0_ref[...], kk_ref[...], ka_ref[...],
           wl_ref[...], al_ref[...], gl_ref[...])
    r, lw, k2, v, kk, b, g = _rwkv_pre(z_ref[...], prev_ref[...], prm, e_ref[...])
    r_o[...] = r
    w_o[...] = jnp.exp(lw)
    k_o[...] = k2
    v_o[...] = v
    kk_o[...] = kk
    b_o[...] = b
    g_o[...] = g


def _rwkv_step_kernel(s_ref, r_ref, w_ref, k_ref, kk_ref, b_ref, v_ref, s_out, o_out):
    s = s_ref[0]
    sa = jnp.sum(s * (-kk_ref[0]), axis=-1, keepdims=True)
    s_new = s * w_ref[0] + sa * b_ref[0] + v_ref[0] * k_ref[0]
    s_out[0] = s_new
    o_out[0] = jnp.sum(s_new * r_ref[0], axis=-1, keepdims=True)


def _rwkv_sample_post_kernel(o_ref, r_ref, k_ref, v_ref, g_ref, rk_ref, lw_ref, lb_ref, e_ref, out_ref):
    out = _rwkv_post(o_ref[...], r_ref[...], k_ref[...], v_ref[...], g_ref[...],
                     rk_ref[...], lw_ref[...], lb_ref[...], e_ref[...])
    out_ref[...] = out.astype(out_ref.dtype)


def _rwkv_sample(zr, prev_pad, s0, rp):
    n = zr.shape[0]
    vec = lambda w: _full((1, w))
    rows = jax.ShapeDtypeStruct((n, RWKV_W), F32)
    r, w, k2, v, kk, b, g = pl.pallas_call(
        _rwkv_sample_pre_kernel,
        grid=(1,),
        in_specs=[_full((n, ZR_W)), _full((n, ZR_W)), vec(ZR_W), vec(RWKV_W), vec(RWKV_W), vec(RWKV_W),
                  vec(RWKV_W), _full((LANES, RWKV_W)), _full((LANES, RWKV_W)), _full((2 * LANES, RWKV_W)),
                  _full((LANES, LANES))],
        out_specs=[_full((n, RWKV_W))] * 7,
        out_shape=[rows] * 7,
        compiler_params=_cparams("arbitrary"),
        name="rwkv_sample_pre",
    )(zr, prev_pad, rp["mu"], rp["w0"], rp["a0"], rp["k_k"], rp["k_a"], rp["wl"], rp["al"], rp["gl"], rp["e"])

    as_row = lambda u: u.reshape(n, RWKV_H, 1, RWKV_HD)
    as_col = lambda u: u.reshape(n, RWKV_H, RWKV_HD, 1)
    row_spec = pl.BlockSpec((1, RWKV_H, 1, RWKV_HD), lambda i: (i, 0, 0, 0))
    col_spec = pl.BlockSpec((1, RWKV_H, RWKV_HD, 1), lambda i: (i, 0, 0, 0))
    st_spec = pl.BlockSpec((1, RWKV_H, RWKV_HD, RWKV_HD), lambda i: (i, 0, 0, 0))
    s_new, o = pl.pallas_call(
        _rwkv_step_kernel,
        grid=(n,),
        in_specs=[st_spec] + [row_spec] * 5 + [col_spec],
        out_specs=[st_spec, col_spec],
        out_shape=[jax.ShapeDtypeStruct((n, RWKV_H, RWKV_HD, RWKV_HD), F32),
                   jax.ShapeDtypeStruct((n, RWKV_H, RWKV_HD, 1), F32)],
        compiler_params=_cparams("parallel"),
        name="rwkv_sample_step",
    )(s0, as_row(r), as_row(w), as_row(k2), as_row(kk), as_row(b), as_col(v))

    out = pl.pallas_call(
        _rwkv_sample_post_kernel,
        grid=(1,),
        in_specs=[_full((n, RWKV_W))] * 5 + [vec(RWKV_W)] * 3 + [_full((LANES, LANES))],
        out_specs=_full((n, RWKV_W)),
        out_shape=jax.ShapeDtypeStruct((n, RWKV_W), BF16),
        compiler_params=_cparams("arbitrary"),
        name="rwkv_sample_post",
    )(o.reshape(n, RWKV_W), r, k2, v, g, rp["r_k"], rp["lnx_w"], rp["lnx_b"], rp["e"])
    return out, s_new


def _block_mean_kernel(k_ref, km_ref):
    x = k_ref[...]
    npg = x.shape[0]
    ppb = MOBA_BLOCK // PAGE_SIZE
    x = x.reshape(npg // ppb, MOBA_BLOCK, MOBA_HD)
    km_ref[0] = jnp.sum(x, axis=1) * (1.0 / MOBA_BLOCK)


def _block_means(k_pages):
    npg = k_pages.shape[0]
    nb = npg * PAGE_SIZE // MOBA_BLOCK
    return pl.pallas_call(
        _block_mean_kernel,
        grid=(MOBA_H,),
        in_specs=[pl.BlockSpec((npg, 1, PAGE_SIZE, MOBA_HD), lambda h: (0, h, 0, 0))],
        out_specs=pl.BlockSpec((1, nb, MOBA_HD), lambda h: (h, 0, 0)),
        out_shape=jax.ShapeDtypeStruct((MOBA_H, nb, MOBA_HD), F32),
        compiler_params=_cparams("parallel"),
        name="moba_block_means",
    )(k_pages)


MOBA_GROUP = 4
MASK_BIAS = -1e30
LOG2E = 1.4426950408889634
MOBA_TILES_PER_STEP = 2
MOBA_VMEM_LIMIT_BYTES = 56 * 1024 * 1024


def _select_bias_t(gate_t, n_valid):
    neg = jnp.float32(-jnp.inf)
    blk = lax.broadcasted_iota(jnp.int32, gate_t.shape, 0).astype(F32)
    g = jnp.where(blk < n_valid, gate_t, neg)
    bias = jnp.full(gate_t.shape, MASK_BIAS, F32)
    for _ in range(MOBA_TOPK):
        mx = jnp.max(g, axis=0, keepdims=True)
        idx = jnp.min(jnp.where(g == mx, blk, jnp.float32(1e9)), axis=0, keepdims=True)
        pick = (blk == idx) & (mx > neg)
        bias = jnp.where(pick, 0.0, bias)
        g = jnp.where(pick, neg, g)
    return bias


def _moba_prompt_kernel(q_ref, km_ref, kb_ref, vb_ref, o_ref, qa_ref, acc_ref, s_ref):
    step = pl.program_id(1)
    ng, tq, _ = q_ref.shape
    nsub = tq // MOBA_BLOCK
    first = step * nsub
    scale = MOBA_HD ** -0.5
    neg = jnp.float32(-jnp.inf)
    rr = lax.broadcasted_iota(jnp.int32, (MOBA_BLOCK, MOBA_BLOCK), 0)
    cc = lax.broadcasted_iota(jnp.int32, (MOBA_BLOCK, MOBA_BLOCK), 1)
    causal = cc <= rr
    ones_v = jnp.ones((MOBA_BLOCK, MOBA_HD), BF16)
    qcol = lax.broadcasted_iota(jnp.int32, (1, tq), 1)
    n_valid = (first + qcol // MOBA_BLOCK).astype(F32)

    heads = range(ng)
    lane_blk = lax.broadcasted_iota(jnp.int32, (MOBA_BLOCK, LANES), 1)

    def scores(g, j):
        st = pl.multiple_of(j * MOBA_BLOCK, MOBA_BLOCK)
        onehot = (lane_blk == j).astype(BF16)
        kj = jnp.concatenate([kb_ref[g, pl.ds(st, MOBA_BLOCK), :], onehot], axis=1)
        return lax.dot_general(qa_ref[g], kj, NT, preferred_element_type=F32)

    def values(g, j):
        st = pl.multiple_of(j * MOBA_BLOCK, MOBA_BLOCK)
        return jnp.concatenate([vb_ref[g, pl.ds(st, MOBA_BLOCK), :], ones_v], axis=1)

    m0 = []
    for g in heads:
        q = q_ref[g]
        bias_t = _select_bias_t(_mm3_nt(km_ref[g], q), n_valid)
        pad = jnp.full((LANES - bias_t.shape[0], tq), MASK_BIAS, F32)
        bias = jnp.concatenate([bias_t, pad], axis=0).T
        qs = (q * (scale * LOG2E)).astype(BF16)
        qa_ref[g] = jnp.concatenate([qs, bias.astype(BF16)], axis=1)
        m_sub = []
        for u in range(nsub):
            rows = slice(u * MOBA_BLOCK, (u + 1) * MOBA_BLOCK)
            st = pl.multiple_of((first + u) * MOBA_BLOCK, MOBA_BLOCK)
            kd = kb_ref[g, pl.ds(st, MOBA_BLOCK), :]
            s = jnp.where(causal, lax.dot_general(qs[rows], kd, NT, preferred_element_type=F32), neg)
            m = jnp.max(s, axis=1, keepdims=True)
            p = jnp.exp2(s - m).astype(BF16)
            m_sub.append(m)
            acc_ref[g, rows, :] = jnp.dot(p, values(g, first + u), preferred_element_type=F32)
        m0.append(jnp.concatenate(m_sub, axis=0))
    for g in heads:
        s_ref[0, g] = scores(g, 0)

    def body(j, ms):
        slot = j & 1
        s_next = [scores(g, j + 1) for g in heads]
        s = [s_ref[slot, g] for g in heads]
        m_new = [jnp.maximum(ms[g], jnp.max(s[g], axis=1, keepdims=True)) for g in heads]
        p = [jnp.exp2(s[g] - m_new[g]).astype(BF16) for g in heads]
        pv = [jnp.dot(p[g], values(g, j), preferred_element_type=F32) for g in heads]
        for g in heads:
            acc_ref[g] = jnp.exp2(ms[g] - m_new[g]) * acc_ref[g] + pv[g]
            s_ref[1 - slot, g] = s_next[g]
        return tuple(m_new)

    lax.fori_loop(0, first + nsub - 1, body, tuple(m0))
    for g in heads:
        acc = acc_ref[g]
        o_ref[:, g * MOBA_HD:(g + 1) * MOBA_HD] = (acc[:, :MOBA_HD] / acc[:, MOBA_HD:]).astype(o_ref.dtype)


def _moba_prompt(q3, km, kb, vb):
    t = q3.shape[1]
    nb = t // MOBA_BLOCK
    tq = MOBA_TILES_PER_STEP * MOBA_BLOCK
    assert t % tq == 0 and nb <= LANES and nb % 8 == 0 and MOBA_H % MOBA_GROUP == 0
    ng = MOBA_GROUP
    resident = lambda: pl.BlockSpec((ng, t, MOBA_HD), lambda h, i: (h, 0, 0), pipeline_mode=pl.Buffered(1))
    return pl.pallas_call(
        _moba_prompt_kernel,
        grid=(MOBA_H // ng, t // tq),
        in_specs=[pl.BlockSpec((ng, tq, MOBA_HD), lambda h, i: (h, i, 0)),
                  pl.BlockSpec((ng, nb, MOBA_HD), lambda h, i: (h, 0, 0)),
                  resident(), resident()],
        out_specs=pl.BlockSpec((tq, ng * MOBA_HD), lambda h, i: (i, h)),
        out_shape=jax.ShapeDtypeStruct((t, MOBA_W), BF16),
        scratch_shapes=[pltpu.VMEM((ng, tq, 2 * MOBA_HD), BF16),
                        pltpu.VMEM((ng, tq, 2 * MOBA_HD), F32),
                        pltpu.VMEM((2, ng, tq, MOBA_BLOCK), F32)],
        compiler_params=_cparams("parallel", "arbitrary", vmem=MOBA_VMEM_LIMIT_BYTES),
        name="moba_prompt",
    )(q3, km, kb, vb)


PAGES_PER_STEP = 16


def _page_sum_kernel(pt_ref, *refs):
    del pt_ref
    k_refs, out_ref = refs[:PAGES_PER_STEP], refs[PAGES_PER_STEP]
    for h in range(MOBA_H):
        rows = [jnp.sum(k_refs[u][0, 0, h], axis=0, keepdims=True) for u in range(PAGES_PER_STEP)]
        out_ref[0, h] = jnp.concatenate(rows, axis=0)


def _page_sums(cache_k, page_table):
    nbt, n_pages = page_table.shape
    assert n_pages % PAGES_PER_STEP == 0
    spec = lambda u: pl.BlockSpec((1, 1, MOBA_H, PAGE_SIZE, MOBA_HD),
                                  lambda b, g, pt: (pt[b, g * PAGES_PER_STEP + u], 0, 0, 0, 0))
    return pl.pallas_call(
        _page_sum_kernel,
        grid_spec=pltpu.PrefetchScalarGridSpec(
            num_scalar_prefetch=1,
            grid=(nbt, n_pages // PAGES_PER_STEP),
            in_specs=[spec(u) for u in range(PAGES_PER_STEP)],
            out_specs=pl.BlockSpec((1, MOBA_H, PAGES_PER_STEP, MOBA_HD), lambda b, g, pt: (b, 0, g, 0))),
        out_shape=jax.ShapeDtypeStruct((nbt, MOBA_H, n_pages, MOBA_HD), F32),
        compiler_params=_cparams("parallel", "arbitrary"),
        name="moba_page_sums",
    )(page_table, *([cache_k] * PAGES_PER_STEP))


def _sample_gate_kernel(q_ref, ps_ref, sel_ref):
    n_pages = ps_ref.shape[2]
    lane = lax.broadcasted_iota(jnp.int32, (1, n_pages), 1)
    even = (lane & 1) == 0
    neg = jnp.float32(-jnp.inf)
    big = jnp.int32(2 ** 30)
    out_lane = lax.broadcasted_iota(jnp.int32, (1, LANES), 1)
    rows = []
    for h in range(MOBA_H):
        qh = jnp.broadcast_to(q_ref[0, h:h + 1, :], (8, MOBA_HD))
        gp = _mm3_nt(qh, ps_ref[0, h])[0:1, :]
        nbr = jnp.where(even, pltpu.roll(gp, n_pages - 1, axis=1), pltpu.roll(gp, 1, axis=1))
        g = jnp.where(even, (gp + nbr) * (1.0 / MOBA_BLOCK), neg)
        res = jnp.zeros((1, LANES), jnp.int32)
        for s in range(MOBA_TOPK):
            mx = jnp.max(g, axis=1, keepdims=True)
            idx = jnp.min(jnp.where(g == mx, lane, big), axis=1, keepdims=True)
            g = jnp.where(lane == idx, neg, g)
            res = jnp.where(out_lane == s, idx >> 1, res)
        rows.append(res)
    sel_ref[0] = jnp.concatenate(rows, axis=0)


def _sample_gate(q3, psums):
    nbt, _, n_pages, _ = psums.shape
    assert n_pages == LANES and n_pages // 2 >= MOBA_TOPK
    return pl.pallas_call(
        _sample_gate_kernel,
        grid=(nbt,),
        in_specs=[pl.BlockSpec((1, MOBA_H, MOBA_HD), lambda b: (b, 0, 0)),
                  pl.BlockSpec((1, MOBA_H, n_pages, MOBA_HD), lambda b: (b, 0, 0, 0))],
        out_specs=pl.BlockSpec((1, MOBA_H, LANES), lambda b: (b, 0, 0)),
        out_shape=jax.ShapeDtypeStruct((nbt, MOBA_H, LANES), jnp.int32),
        compiler_params=_cparams("parallel"),
        name="moba_sample_gate",
    )(q3, psums)


SAMPLE_PAGES = MOBA_TOPK * (MOBA_BLOCK // PAGE_SIZE)


def _sample_attn_kernel(ph_ref, q_ref, kn_ref, vn_ref, *refs):
    del ph_ref
    n_in = MOBA_H * SAMPLE_PAGES
    k_refs, v_refs, o_ref = refs[:n_in], refs[n_in:2 * n_in], refs[2 * n_in]
    scale = MOBA_HD ** -0.5
    for h in range(MOBA_H):
        q = q_ref[0, h]
        kp = k_refs[h * SAMPLE_PAGES:(h + 1) * SAMPLE_PAGES]
        vp = v_refs[h * SAMPLE_PAGES:(h + 1) * SAMPLE_PAGES]
        s_self = jnp.sum(kn_ref[0, h] * q, axis=1, keepdims=True) * scale
        ss = [jnp.sum(kp[u][0, 0, 0] * q, axis=1, keepdims=True) * scale for u in range(SAMPLE_PAGES)]
        m = s_self
        for s in ss:
            m = jnp.maximum(m, jnp.max(s, axis=0, keepdims=True))
        p_self = jnp.exp(s_self - m)
        l = p_self
        acc = p_self * vn_ref[0, h]
        for u in range(SAMPLE_PAGES):
            p = jnp.exp(ss[u] - m)
            l = l + jnp.sum(p, axis=0, keepdims=True)
            acc = acc + jnp.sum(p * vp[u][0, 0, 0], axis=0, keepdims=True)
        o_ref[0, h] = acc / l


def _sample_attn(q4, kn4, vn4, cache_k, cache_v, phys):
    nbt = q4.shape[0]
    vec_spec = pl.BlockSpec((1, MOBA_H, 1, MOBA_HD), lambda b, ph: (b, 0, 0, 0))
    page_spec = lambda h, u: pl.BlockSpec(
        (1, 1, 1, PAGE_SIZE, MOBA_HD),
        lambda b, ph: (ph[(b * MOBA_H + h) * SAMPLE_PAGES + u], 0, h, 0, 0))
    pages = [page_spec(h, u) for h in range(MOBA_H) for u in range(SAMPLE_PAGES)]
    return pl.pallas_call(
        _sample_attn_kernel,
        grid_spec=pltpu.PrefetchScalarGridSpec(
            num_scalar_prefetch=1,
            grid=(nbt,),
            in_specs=[vec_spec, vec_spec, vec_spec] + pages + pages,
            out_specs=vec_spec),
        out_shape=jax.ShapeDtypeStruct((nbt, MOBA_H, 1, MOBA_HD), F32),
        compiler_params=_cparams("arbitrary"),
        name="moba_sample_attn",
    )(phys, q4, kn4, vn4, *([cache_k] * len(pages)), *([cache_v] * len(pages)))


def _outproj_kernel(x_ref, or_ref, om_ref, wr_ref, wm_ref, g_ref, o_ref):
    y = (jnp.dot(or_ref[...], wr_ref[...], preferred_element_type=F32)
         + jnp.dot(om_ref[...], wm_ref[...], preferred_element_type=F32))
    o_ref[...] = x_ref[...] + _rms(y, g_ref[...])


def _outproj(x2, o_r, o_m, w_r, w_m, g):
    m = x2.shape[0]
    tm = min(m, 512)
    assert m % tm == 0
    return pl.pallas_call(
        _outproj_kernel,
        grid=(m // tm,),
        in_specs=[pl.BlockSpec((tm, D_MODEL), lambda i: (i, 0)),
                  pl.BlockSpec((tm, RWKV_W), lambda i: (i, 0)),
                  pl.BlockSpec((tm, MOBA_W), lambda i: (i, 0)),
                  _full((RWKV_W, D_MODEL)), _full((MOBA_W, D_MODEL)), _full((1, D_MODEL))],
        out_specs=pl.BlockSpec((tm, D_MODEL), lambda i: (i, 0)),
        out_shape=jax.ShapeDtypeStruct((m, D_MODEL), F32),
        compiler_params=_cparams("parallel"),
        name="outproj",
    )(x2, o_r, o_m, w_r, w_m, g)


MLP_TF = 1024
MLP_TM = 512


def _mlp_kernel(x_ref, g3_ref, g4_ref, wu_ref, wd_ref, o_ref, xn_ref, acc_ref):
    j = pl.program_id(1)

    @pl.when(j == 0)
    def _():
        xn_ref[...] = _rms(x_ref[...], g3_ref[...]).astype(BF16)
        acc_ref[...] = jnp.zeros_like(acc_ref)

    h = jnp.dot(xn_ref[...], wu_ref[...], preferred_element_type=F32)
    h = jnp.square(jnp.maximum(h, 0.0))
    acc_ref[...] += jnp.dot(h.astype(BF16), wd_ref[...], preferred_element_type=F32)

    @pl.when(j == pl.num_programs(1) - 1)
    def _():
        o_ref[...] = x_ref[...] + _rms(acc_ref[...], g4_ref[...])


def _mlp(x1, g3, g4, w_up, w_down):
    m = x1.shape[0]
    tm = min(m, MLP_TM)
    assert m % tm == 0
    return pl.pallas_call(
        _mlp_kernel,
        grid=(m // tm, D_FF // MLP_TF),
        in_specs=[pl.BlockSpec((tm, D_MODEL), lambda i, j: (i, 0)),
                  pl.BlockSpec((1, D_MODEL), lambda i, j: (0, 0)),
                  pl.BlockSpec((1, D_MODEL), lambda i, j: (0, 0)),
                  pl.BlockSpec((D_MODEL, MLP_TF), lambda i, j: (0, j)),
                  pl.BlockSpec((MLP_TF, D_MODEL), lambda i, j: (j, 0))],
        out_specs=pl.BlockSpec((tm, D_MODEL), lambda i, j: (i, 0)),
        out_shape=jax.ShapeDtypeStruct((m, D_MODEL), F32),
        scratch_shapes=[pltpu.VMEM((tm, D_MODEL), BF16), pltpu.VMEM((tm, D_MODEL), F32)],
        compiler_params=_cparams("parallel", "arbitrary"),
        name="mlp",
    )(x1, g3, g4, w_up, w_down)


def _pad_cols(a, lora_only=False):
    z = lambda n: jnp.zeros(a.shape[:-1] + (n,), a.dtype)
    lo = 0 if lora_only else 3 * RWKV_W
    wd = a[..., lo:lo + DECAY_LORA]
    ad = a[..., lo + DECAY_LORA:lo + DECAY_LORA + AAA_LORA]
    gd = a[..., lo + DECAY_LORA + AAA_LORA:lo + DECAY_LORA + AAA_LORA + GATE_LORA]
    parts = [wd, z(LANES - DECAY_LORA), ad, z(LANES - AAA_LORA), gd, z(2 * LANES - GATE_LORA)]
    if not lora_only:
        parts.insert(0, a[..., :lo])
    return jnp.concatenate(parts, axis=-1)


def _unpad_cols(zr):
    return jnp.concatenate([zr[..., :3 * RWKV_W + DECAY_LORA], zr[..., AD_OFF:AD_OFF + AAA_LORA],
                            zr[..., GD_OFF:GD_OFF + GATE_LORA]], axis=-1)


def _pad_rows(a, n):
    return jnp.concatenate([a, jnp.zeros((n - a.shape[0],) + a.shape[1:], a.dtype)], axis=0)


def _unpair_state(s_bd):
    h0 = s_bd[:, :RWKV_HD, :RWKV_HD]
    h1 = s_bd[:, RWKV_HD:, RWKV_HD:]
    return jnp.stack([h0, h1], axis=1).reshape(RWKV_H, RWKV_HD, RWKV_HD)


def kernel(x_prompt, x_sample, cache_k, cache_v, page_table, state_wkv, state_shift, g_mix_pre, g_mix_post, g_ffn_pre, g_ffn_post, w_in, mu_shift, w0, w_lora2, a0, a_lora2, g_lora2, k_k, k_a, r_k, lnx_w, lnx_b, w_out, w_up, w_down):
    depth = w_in.shape[0]
    bsz, seq, _ = x_prompt.shape
    dbt, dseq, _ = x_sample.shape
    assert depth == 1 and bsz == 1 and dseq == 1, "kernels are written for the stated shapes"
    n_pages = page_table.shape[1]
    l = 0

    row = lambda a: a.reshape(1, -1)
    w_rkv = w_in[l, :, :3 * RWKV_W].astype(BF16)
    w_lora = _pad_cols(w_in[l, :, 3 * RWKV_W:SHIFT_W].astype(BF16), lora_only=True)
    w_moba = w_in[l, :, SHIFT_W:].astype(BF16)
    lane = jnp.arange(LANES)
    e_seg = ((lane[:, None] // RWKV_HD) == (lane[None, :] // RWKV_HD)).astype(BF16)
    ci = jnp.arange(CHUNK)
    tri = (ci[:, None] >= ci[None, :]).astype(BF16)
    rp = dict(mu=row(_pad_cols(mu_shift[l])), w0=row(w0[l]), a0=row(a0[l]), k_k=row(k_k[l]),
              k_a=row(k_a[l]), r_k=row(r_k[l]), lnx_w=row(lnx_w[l]), lnx_b=row(lnx_b[l]),
              wl=_pad_rows(w_lora2[l], LANES).astype(BF16), al=_pad_rows(a_lora2[l], LANES).astype(BF16),
              gl=_pad_rows(g_lora2[l], 2 * LANES).astype(BF16), e=e_seg, tri=tri)
    w_or = w_out[l, :RWKV_W].astype(BF16)
    w_om = w_out[l, RWKV_W:].astype(BF16)
    w_up_b = w_up[l].astype(BF16)
    w_down_b = w_down[l].astype(BF16)
    g1, g2, g3, g4 = row(g_mix_pre[l]), row(g_mix_post[l]), row(g_ffn_pre[l]), row(g_ffn_post[l])

    xp = x_prompt.reshape(seq, D_MODEL)
    zr, q3, k_pages, v_pages, kb, vb = _proj(xp, g1, w_rkv, w_lora, w_moba, paged=True)
    o_r, s_bd = _rwkv_prompt(zr, rp)
    km = _block_means(k_pages)
    o_m = _moba_prompt(q3, km, kb, vb)
    x1 = _outproj(xp, o_r, o_m, w_or, w_om, g2)
    y_prompt = _mlp(x1, g3, g4, w_up_b, w_down_b).reshape(bsz, seq, D_MODEL)
    npg = seq // PAGE_SIZE
    k_prompt = k_pages.reshape(bsz, npg, 1, MOBA_H, PAGE_SIZE, MOBA_HD)
    v_prompt = v_pages.reshape(bsz, npg, 1, MOBA_H, PAGE_SIZE, MOBA_HD)
    wkv_prompt = _unpair_state(s_bd).reshape(1, bsz, RWKV_H, RWKV_HD, RWKV_HD)
    shift_prompt = _unpad_cols(zr[seq - 1:seq]).reshape(1, bsz, SHIFT_W)

    xs = x_sample.reshape(dbt, D_MODEL)
    zr_s, q_s, k_s, v_s = _proj(xs, g1, w_rkv, w_lora, w_moba, paged=False)
    o_rs, wkv_s = _rwkv_sample(zr_s, _pad_cols(state_shift[l]), state_wkv[l], rp)
    psums = _page_sums(cache_k, page_table)
    sel = _sample_gate(q_s.reshape(dbt, MOBA_H, MOBA_HD), psums)[:, :, :MOBA_TOPK]
    ppb = MOBA_BLOCK // PAGE_SIZE
    logical = sel[..., None] * ppb + jnp.arange(ppb, dtype=jnp.int32)
    phys = jnp.take_along_axis(page_table, logical.reshape(dbt, -1), axis=1).reshape(-1)
    as4 = lambda u: u.reshape(dbt, MOBA_H, 1, MOBA_HD)
    o_ms = _sample_attn(as4(q_s), as4(k_s), as4(v_s), cache_k, cache_v, phys)
    o_ms = o_ms.reshape(dbt, MOBA_W).astype(BF16)
    x1s = _outproj(xs, o_rs, o_ms, w_or, w_om, g2)
    y_sample = _mlp(x1s, g3, g4, w_up_b, w_down_b).reshape(dbt, dseq, D_MODEL)
    k_sample = k_s.reshape(dbt, 1, MOBA_H, dseq, MOBA_HD)
    v_sample = v_s.reshape(dbt, 1, MOBA_H, dseq, MOBA_HD)
    wkv_sample = wkv_s.reshape(1, dbt, RWKV_H, RWKV_HD, RWKV_HD)
    shift_sample = _unpad_cols(zr_s).reshape(1, dbt, SHIFT_W)

    return (y_prompt, y_sample, k_prompt, v_prompt, wkv_prompt, shift_prompt,
            k_sample, v_sample, wkv_sample, shift_sample)
```
